```python
import math
import jax, jax.numpy as jnp
from jax import lax
import numpy as np

D_MODEL = 1024
BATCH = 1
SEQ = 16384
DEPTH = 1

N_META = 16
NORM_EPS = 1e-6

DA_HEADS = 8
DA_HEAD_DIM = 64
DA_V_DIM = 2 * DA_HEAD_DIM
DA_QK = DA_HEADS * 2 * DA_HEAD_DIM
DA_V = DA_HEADS * DA_V_DIM
Q_BLOCK = 128
ALIBI_SLOPES = tuple(2.0 ** (-8.0 * (h + 1) / DA_HEADS) for h in range(DA_HEADS))

DN_HEADS = 8
DN_HEAD_DIM = 128
DN_QK = DN_HEADS * DN_HEAD_DIM
DN_V = DN_HEADS * DN_HEAD_DIM
DN_CONV = 4
DN_CHUNK = 64

N_GROUPS = 4
EXPERTS_PER_GROUP = 8
N_EXPERTS = N_GROUPS * EXPERTS_PER_GROUP
D_EXPERT = 256
TOP_K = 2

IN_SIZES = (DA_QK, DA_QK, DA_V, DN_QK, DN_QK, DN_V, DN_V, DN_HEADS, DN_HEADS, D_MODEL, D_MODEL)
IN_OFFSETS = tuple(int(s) for s in np.cumsum(IN_SIZES)[:-1])
W_IN_COLS = int(sum(IN_SIZES))

kernel_name = "hybrid_diffattn_gdn_hmoe_block"

F32 = jnp.float32


def rms_norm(x, w):
    xf = x.astype(F32)
    y = xf * lax.rsqrt(jnp.mean(xf * xf, axis=-1, keepdims=True) + NORM_EPS)
    return (y * w.astype(F32)).astype(x.dtype)


def l2_normalize(x):
    return x * lax.rsqrt(jnp.sum(x * x, axis=-1, keepdims=True) + 1e-6)


def differential_attention(q, k, v, lq1, lk1, lq2, lk2, subln_w, lambda_init):
    B, L, _ = q.shape
    n_blk = -(-L // Q_BLOCK)
    Lp = n_blk * Q_BLOCK
    pad = ((0, 0), (0, Lp - L), (0, 0))
    qb = jnp.pad(q, pad).reshape(B, n_blk, Q_BLOCK, DA_HEADS, 2, DA_HEAD_DIM)
    qb = jnp.moveaxis(qb, 1, 0)
    kk = jnp.pad(k, pad).reshape(B, Lp, DA_HEADS, 2, DA_HEAD_DIM)
    vv = jnp.pad(v, pad).reshape(B, Lp, DA_HEADS, DA_V_DIM)
    lam = (jnp.exp(jnp.sum(lq1.astype(F32) * lk1.astype(F32)))
           - jnp.exp(jnp.sum(lq2.astype(F32) * lk2.astype(F32))) + lambda_init)
    slopes = jnp.asarray(ALIBI_SLOPES, dtype=F32)[:, None, None]
    k_pos = jnp.arange(Lp, dtype=jnp.int32)
    scale = DA_HEAD_DIM ** -0.5

    def attend_block(args):
        q_blk, start = args
        q_pos = start + jnp.arange(Q_BLOCK, dtype=jnp.int32)
        dist = (q_pos[:, None] - k_pos[None, :]).astype(F32)
        bias = jnp.where(dist >= 0, -slopes * dist, -jnp.inf)
        s = jnp.einsum('bqhmd,bkhmd->bhmqk', q_blk, kk, preferred_element_type=F32) * scale
        p = jax.nn.softmax(s + bias[None, :, None], axis=-1)
        w = p[:, :, 0] - lam * p[:, :, 1]
        return jnp.einsum('bhqk,bkhe->bqhe', w.astype(vv.dtype), vv, preferred_element_type=F32)

    starts = jnp.arange(n_blk, dtype=jnp.int32) * Q_BLOCK
    o = lax.map(attend_block, (qb, starts))
    o = jnp.moveaxis(o, 0, 1).reshape(B, Lp, DA_HEADS, DA_V_DIM)[:, :L]
    o = rms_norm(o, subln_w) * (1.0 - lambda_init)
    return o.reshape(B, L, DA_V).astype(v.dtype)


def chunk_gated_delta_rule(q, k, v, beta, g):
    B, Lc, H, _ = q.shape
    dv = v.shape[-1]
    C = DN_CHUNK
    N = Lc // C

    def to_chunks(t):
        return jnp.moveaxis(t.reshape((B, N, C, H) + t.shape[3:]), 3, 1)

    q, k, v, beta, g = (to_chunks(t) for t in (q, k, v, beta, g))
    gc = jnp.cumsum(g, axis=-1)
    incl = jnp.tril(jnp.ones((C, C), dtype=bool))
    strict = jnp.tril(jnp.ones((C, C), dtype=bool), -1)
    diff = gc[..., :, None] - gc[..., None, :]
    decay = jnp.where(incl, jnp.exp(jnp.where(incl, diff, 0.0)), 0.0)
    kb = k * beta[..., None]
    A = jnp.where(strict, jnp.einsum('bhnid,bhnjd->bhnij', kb, k) * decay, 0.0)
    eye = jnp.eye(C, dtype=F32)
    T = lax.linalg.triangular_solve(A + eye, jnp.broadcast_to(eye, A.shape),
                                    left_side=True, lower=True, unit_diagonal=True)
    u = jnp.einsum('bhnij,bhnje->bhnie', T, v * beta[..., None])
    w = jnp.einsum('bhnij,bhnjd->bhnid', T, kb * jnp.exp(gc)[..., None])
    qk = jnp.einsum('bhnid,bhnjd->bhnij', q, k) * decay
    q_dec = q * jnp.exp(gc)[..., None]
    k_dec = k * jnp.exp(gc[..., -1:] - gc)[..., None]
    g_tot = jnp.exp(gc[..., -1])
    xs = tuple(jnp.moveaxis(t, 2, 0) for t in (u, w, q_dec, k_dec, qk, g_tot))

    def step(S, inp):
        u_n, w_n, qd_n, kd_n, qk_n, gt_n = inp
        v_new = u_n - jnp.einsum('bhcd,bhde->bhce', w_n, S)
        o_n = jnp.einsum('bhcd,bhde->bhce', qd_n, S) + jnp.einsum('bhij,bhje->bhie', qk_n, v_new)
        S = S * gt_n[..., None, None] + jnp.einsum('bhcd,bhce->bhde', kd_n, v_new)
        return S, o_n

    S0 = jnp.zeros((B, H, q.shape[-1], dv), dtype=F32)
    _, o = lax.scan(step, S0, xs)
    o = jnp.moveaxis(o, 0, 2)
    return jnp.moveaxis(o, 1, 3).reshape(B, Lc, H, dv)


def gated_deltanet(q, k, v, gate, beta_logit, a_logit, conv_w, A_log, dt_bias, norm_w):
    B, L, _ = q.shape
    qkv = jnp.concatenate([q, k, v], axis=-1)
    ch = qkv.shape[-1]
    qkv = lax.conv_general_dilated(qkv, conv_w[:, None, :].astype(qkv.dtype), (1,), [(DN_CONV - 1, 0)],
                                   dimension_numbers=('NWC', 'WIO', 'NWC'), feature_group_count=ch)
    qkv = jax.nn.silu(qkv.astype(F32))
    qc, kc, vc = jnp.split(qkv, [DN_QK, 2 * DN_QK], axis=-1)
    qc = l2_normalize(qc.reshape(B, L, DN_HEADS, DN_HEAD_DIM)) * (DN_HEAD_DIM ** -0.5)
    kc = l2_normalize(kc.reshape(B, L, DN_HEADS, DN_HEAD_DIM))
    vc = vc.reshape(B, L, DN_HEADS, DN_HEAD_DIM)
    beta = jax.nn.sigmoid(beta_logit.astype(F32))
    g = -jnp.exp(A_log.astype(F32)) * jax.nn.softplus(a_logit.astype(F32) + dt_bias.astype(F32))
    front = (-N_META) % DN_CHUNK
    total = front + L
    back = (-total) % DN_CHUNK

    def pad(t):
        return jnp.pad(t, ((0, 0), (front, back)) + ((0, 0),) * (t.ndim - 2))

    o = chunk_gated_delta_rule(pad(qc), pad(kc), pad(vc), pad(beta), pad(g))[:, front:front + L]
    o = rms_norm(o, norm_w) * jax.nn.silu(gate.astype(F32).reshape(B, L, DN_HEADS, DN_HEAD_DIM))
    return o.reshape(B, L, DN_V).astype(gate.dtype)


def hierarchical_moe(x, wg_r, bg_r, we_r, be_r, w_gate, w_up, w_down):
    B, L, D = x.shape
    xt = x.reshape(B * L, D)
    g_prob = jax.nn.softmax((xt @ wg_r).astype(F32) + bg_r.astype(F32), axis=-1)
    g_w, g_i = lax.top_k(g_prob, 1)
    e_logit = ((xt @ we_r).astype(F32) + be_r.astype(F32)).reshape(-1, N_GROUPS, EXPERTS_PER_GROUP)
    e_logit = jnp.einsum('tg,tge->te', jax.nn.one_hot(g_i[:, 0], N_GROUPS, dtype=F32), e_logit)
    e_prob = jax.nn.softmax(e_logit, axis=-1)
    e_w, e_i = lax.top_k(e_prob, TOP_K)
    e_w = e_w / jnp.sum(e_w, axis=-1, keepdims=True)
    weights = g_w * e_w
    expert_ids = g_i * EXPERTS_PER_GROUP + e_i
    combine = jnp.sum(jax.nn.one_hot(expert_ids, N_EXPERTS, dtype=F32) * weights[..., None], axis=1)
    y = jnp.zeros((B * L, D), dtype=F32)
    for e in range(N_EXPERTS):
        h = jax.nn.silu(xt @ w_gate[e]) * (xt @ w_up[e])
        y = y + combine[:, e:e + 1] * (h @ w_down[e]).astype(F32)
    return y.reshape(B, L, D).astype(x.dtype)


def setup_inputs(seed: int = 0) -> dict:
    key = jax.random.key(seed)
    ks = jax.random.split(key, 26)
    nrm = jax.random.normal
    Ld = DEPTH
    dt = jnp.exp(jax.random.uniform(ks[11], (Ld, DN_HEADS)) * (math.log(0.1) - math.log(0.001)) + math.log(0.001))
    return {
        "x": nrm(ks[0], (BATCH, SEQ, D_MODEL), F32),
        "meta_tokens": nrm(ks[1], (N_META, D_MODEL), F32),
        "norm1_w": 1.0 + 0.02 * nrm(ks[2], (Ld, D_MODEL), F32),
        "w_in": nrm(ks[3], (Ld, D_MODEL, W_IN_COLS), F32) * D_MODEL ** -0.5,
        "lambda_q1": 0.1 * nrm(ks[4], (Ld, DA_HEAD_DIM), F32),
        "lambda_k1": 0.1 * nrm(ks[5], (Ld, DA_HEAD_DIM), F32),
        "lambda_q2": 0.1 * nrm(ks[6], (Ld, DA_HEAD_DIM), F32),
        "lambda_k2": 0.1 * nrm(ks[7], (Ld, DA_HEAD_DIM), F32),
        "da_subln_w": 1.0 + 0.02 * nrm(ks[8], (Ld, DA_V_DIM), F32),
        "dn_conv_w": nrm(ks[9], (Ld, DN_CONV, 2 * DN_QK + DN_V), F32) * DN_CONV ** -0.5,
        "dn_A_log": jnp.log(jax.random.uniform(ks[10], (Ld, DN_HEADS), F32, 1.0, 16.0)),
        "dn_dt_bias": dt + jnp.log(-jnp.expm1(-dt)),
        "dn_norm_w": 1.0 + 0.02 * nrm(ks[12], (Ld, DN_HEAD_DIM), F32),
        "w_branch_attn": nrm(ks[13], (Ld, DA_V, D_MODEL), F32) * DA_V ** -0.5,
        "w_branch_dn": nrm(ks[14], (Ld, DN_V, D_MODEL), F32) * DN_V ** -0.5,
        "w_out": nrm(ks[15], (Ld, D_MODEL, D_MODEL), F32) * D_MODEL ** -0.5,
        "norm2_w": 1.0 + 0.02 * nrm(ks[16], (Ld, D_MODEL), F32),
        "router_group_w": nrm(ks[17], (Ld, D_MODEL, N_GROUPS), F32) * D_MODEL ** -0.5,
        "router_group_b": 0.01 * nrm(ks[18], (Ld, N_GROUPS), F32),
        "router_expert_w": nrm(ks[19], (Ld, D_MODEL, N_EXPERTS), F32) * D_MODEL ** -0.5,
        "router_expert_b": 0.01 * nrm(ks[20], (Ld, N_EXPERTS), F32),
        "expert_w_gate": nrm(ks[21], (Ld, N_EXPERTS, D_MODEL, D_EXPERT), F32) * D_MODEL ** -0.5,
        "expert_w_up": nrm(ks[22], (Ld, N_EXPERTS, D_MODEL, D_EXPERT), F32) * D_MODEL ** -0.5,
        "expert_w_down": nrm(ks[23], (Ld, N_EXPERTS, D_EXPERT, D_MODEL), F32) * D_EXPERT ** -0.5,
        "final_norm_w": 1.0 + 0.02 * nrm(ks[24], (D_MODEL,), F32),
    }


def reference(x, meta_tokens, norm1_w, w_in, lambda_q1, lambda_k1, lambda_q2, lambda_k2, da_subln_w,
              dn_conv_w, dn_A_log, dn_dt_bias, dn_norm_w, w_branch_attn, w_branch_dn, w_out, norm2_w,
              router_group_w, router_group_b, router_expert_w, router_expert_b,
              expert_w_gate, expert_w_up, expert_w_down, final_norm_w):
    B = x.shape[0]
    meta = jnp.broadcast_to(meta_tokens[None].astype(x.dtype), (B, N_META, D_MODEL))
    h = jnp.concatenate([meta, x], axis=1)
    for layer in range(DEPTH):
        lambda_init = 0.8 - 0.6 * math.exp(-0.3 * layer)
        u = rms_norm(h, norm1_w[layer])
        proj = jnp.einsum('bld,dn->bln', u, w_in[layer])
        (a_q, a_k, a_v, d_q, d_k, d_v, d_gate, d_beta, d_a, gate_a, gate_d) = jnp.split(proj, IN_OFFSETS, axis=-1)
        y_a = differential_attention(a_q, a_k, a_v, lambda_q1[layer], lambda_k1[layer], lambda_q2[layer],
                                     lambda_k2[layer], da_subln_w[layer], lambda_init)
        y_d = gated_deltanet(d_q, d_k, d_v, d_gate, d_beta, d_a, dn_conv_w[layer], dn_A_log[layer],
                             dn_dt_bias[layer], dn_norm_w[layer])
        y_a = jnp.einsum('ble,ed->bld', y_a, w_branch_attn[layer])
        y_d = jnp.einsum('ble,ed->bld', y_d, w_branch_dn[layer])
        mixed = jax.nn.sigmoid(gate_a) * y_a + jax.nn.sigmoid(gate_d) * y_d
        h = h + jnp.einsum('bld,de->ble', mixed, w_out[layer])
        h = h + hierarchical_moe(rms_norm(h, norm2_w[layer]), router_group_w[layer], router_group_b[layer],
                                 router_expert_w[layer], router_expert_b[layer], expert_w_gate[layer],
                                 expert_w_up[layer], expert_w_down[layer])
    h = rms_norm(h, final_norm_w)
    return h[:, N_META:]
```

```python
import functools
import math

import jax
import jax.numpy as jnp
from jax import lax
from jax.experimental import pallas as pl
from jax.experimental.pallas import tpu as pltpu

F32 = jnp.float32
BF16 = jnp.bfloat16
HIGHEST = lax.Precision.HIGHEST

D_MODEL = 1024
N_META = 16
NORM_EPS = 1e-6
N_HEADS = 8
HEAD_W = 128
DA_HEAD_DIM = 64
DN_CONV = 4
DN_CHUNK = 64
N_GROUPS = 4
EXPERTS_PER_GROUP = 8
N_EXPERTS = 32
D_EXPERT = 256
LAMBDA_INIT = 0.8 - 0.6 * math.exp(-0.3 * 0)
ALIBI_SLOPES = tuple(2.0 ** (-8.0 * (h + 1) / N_HEADS) for h in range(N_HEADS))

TILE = 512
DN_ROWS = 256
LANES = 128
NEG = -1e30
VMEM_LIMIT = 56 * 1024 * 1024

CB_AQ, CB_AK, CB_AV, CB_DQ, CB_DK, CB_DV, CB_DG, CB_GA, CB_GD = 0, 8, 16, 24, 32, 40, 48, 56, 64
PROJ_COLS = 72 * 128


def _cparams(sem):
    return pltpu.CompilerParams(dimension_semantics=sem, vmem_limit_bytes=VMEM_LIMIT)


def _mm(a, b):
    return jnp.dot(a.astype(BF16), b.astype(BF16), preferred_element_type=F32)


def _mm_nt(a, b):
    return lax.dot_general(a.astype(BF16), b.astype(BF16), (((1,), (1,)), ((), ())),
                           preferred_element_type=F32)


def _mm_tn(a, b):
    return lax.dot_general(a.astype(BF16), b.astype(BF16), (((0,), (0,)), ((), ())),
                           preferred_element_type=F32)


def _mm_f32(a, b):
    return jnp.dot(a, b, precision=HIGHEST, preferred_element_type=F32)


def _silu(x):
    return x * (1.0 / (1.0 + jnp.exp(-x)))


def _sigmoid(x):
    return 1.0 / (1.0 + jnp.exp(-x))


def _inproj_kernel(x_ref, nw_ref, w_ref, ws_ref, o_ref, os_ref, u_scr):
    @pl.when(pl.program_id(1) == 0)
    def _():
        x = x_ref[...]
        u = x * lax.rsqrt(jnp.mean(x * x, axis=-1, keepdims=True) + NORM_EPS) * nw_ref[...]
        u_scr[...] = u.astype(BF16)
        os_ref[...] = _mm_f32(u, ws_ref[...])

    o_ref[...] = jnp.dot(u_scr[...], w_ref[...], preferred_element_type=F32).astype(BF16)


def _inproj(hp, norm_w, w_big, w_small):
    lt = hp.shape[0]
    tn = 1024
    return pl.pallas_call(
        _inproj_kernel,
        grid=(lt // TILE, PROJ_COLS // tn),
        in_specs=[
            pl.BlockSpec((TILE, D_MODEL), lambda m, n: (m, 0)),
            pl.BlockSpec((1, D_MODEL), lambda m, n: (0, 0)),
            pl.BlockSpec((D_MODEL, tn), lambda m, n: (0, n)),
            pl.BlockSpec((D_MODEL, LANES), lambda m, n: (0, 0)),
        ],
        out_specs=[
            pl.BlockSpec((TILE, tn), lambda m, n: (m, n)),
            pl.BlockSpec((TILE, LANES), lambda m, n: (m, 0)),
        ],
        out_shape=[
            jax.ShapeDtypeStruct((lt, PROJ_COLS), BF16),
            jax.ShapeDtypeStruct((lt, LANES), F32),
        ],
        scratch_shapes=[pltpu.VMEM((TILE, D_MODEL), BF16)],
        compiler_params=_cparams(("parallel", "arbitrary")),
        name="inproj",
    )(hp, norm_w, w_big, w_small)


def _attn_kernel(q_ref, k_ref, v_ref, slope_ref, lam_ref, sw_ref, o_ref,
                 m1_ref, l1_ref, a1_ref, m2_ref, l2_ref, a2_ref, *, first_key):
    h = pl.program_id(0)
    qi = pl.program_id(1)
    tq = q_ref.shape[0]
    tk = tq

    q = q_ref[...]
    lane = lax.broadcasted_iota(jnp.int32, q.shape, 1)
    zero = jnp.zeros_like(q)
    q1 = jnp.where(lane < DA_HEAD_DIM, q, zero)
    q2 = jnp.where(lane >= DA_HEAD_DIM, q, zero)
    slope = slope_ref[pl.ds(h, 1), :][:, :1]
    row = lax.broadcasted_iota(jnp.int32, (tq, tk), 0)
    col = lax.broadcasted_iota(jnp.int32, (tq, tk), 1)
    rel0 = (col - row).astype(F32)

    m1_ref[...] = jnp.full(m1_ref.shape, NEG, F32)
    m2_ref[...] = jnp.full(m2_ref.shape, NEG, F32)
    l1_ref[...] = jnp.zeros(l1_ref.shape, F32)
    l2_ref[...] = jnp.zeros(l2_ref.shape, F32)
    a1_ref[...] = jnp.zeros(a1_ref.shape, F32)
    a2_ref[...] = jnp.zeros(a2_ref.shape, F32)

    def update(s, vb, m_ref, l_ref, a_ref):
        m_old = m_ref[...]
        m_new = jnp.maximum(m_old, jnp.max(s, axis=-1, keepdims=True))
        p = jnp.exp(s - m_new)
        alpha = jnp.exp(m_old - m_new)
        l_ref[...] = alpha * l_ref[...] + jnp.sum(p, axis=-1, keepdims=True)
        a_ref[...] = alpha * a_ref[...] + jnp.dot(p.astype(BF16), vb, preferred_element_type=F32)
        m_ref[...] = m_new

    def tile(ki, masked):
        start = pl.multiple_of(ki * tk, tk)
        kb = k_ref[pl.ds(start, tk), :]
        vb = v_ref[pl.ds(start, tk), :]
        rel = rel0 + ((ki - qi) * tk).astype(F32)
        bias = slope * rel
        s1 = _mm_nt(q1, kb) + bias
        s2 = _mm_nt(q2, kb) + bias
        if masked:
            valid = (rel <= 0.0) & ((col + ki * tk) >= first_key)
            s1 = jnp.where(valid, s1, NEG)
            s2 = jnp.where(valid, s2, NEG)
        update(s1, vb, m1_ref, l1_ref, a1_ref)
        update(s2, vb, m2_ref, l2_ref, a2_ref)

    tile(0, True)

    def body(ki, carry):
        tile(ki, False)
        return carry

    lax.fori_loop(1, qi, body, 0)

    @pl.when(qi > 0)
    def _():
        tile(qi, True)

    lt = lam_ref[...]
    lam = (jnp.exp(jnp.sum(lt[0:1] * lt[1:2], axis=-1, keepdims=True))
           - jnp.exp(jnp.sum(lt[2:3] * lt[3:4], axis=-1, keepdims=True)) + LAMBDA_INIT)
    o = a1_ref[...] / l1_ref[...] - lam * (a2_ref[...] / l2_ref[...])
    o = o * lax.rsqrt(jnp.mean(o * o, axis=-1, keepdims=True) + NORM_EPS) * sw_ref[...]
    o_ref[...] = (o * (1.0 - LAMBDA_INIT)).astype(BF16)


def _attention(proj, slopes, lam_tab, subln_w, first_key):
    lt = proj.shape[0]
    tq = TILE
    kern = functools.partial(_attn_kernel, first_key=first_key)
    small = lambda: pltpu.VMEM((tq, 1), F32)
    acc = lambda: pltpu.VMEM((tq, HEAD_W), F32)
    return pl.pallas_call(
        kern,
        grid=(N_HEADS, lt // tq),
        in_specs=[
            pl.BlockSpec((tq, HEAD_W), lambda h, i: (i, CB_AQ + h)),
            pl.BlockSpec((lt, HEAD_W), lambda h, i: (0, CB_AK + h)),
            pl.BlockSpec((lt, HEAD_W), lambda h, i: (0, CB_AV + h)),
            pl.BlockSpec((8, LANES), lambda h, i: (0, 0)),
            pl.BlockSpec((8, LANES), lambda h, i: (0, 0)),
            pl.BlockSpec((1, HEAD_W), lambda h, i: (0, 0)),
        ],
        out_specs=pl.BlockSpec((tq, HEAD_W), lambda h, i: (i, h)),
        out_shape=jax.ShapeDtypeStruct((lt, N_HEADS * HEAD_W), BF16),
        scratch_shapes=[small(), small(), acc(), small(), small(), acc()],
        compiler_params=_cparams(("parallel", "arbitrary")),
        name="diff_attn",
    )(proj, proj, proj, slopes, lam_tab, subln_w)


def _dn_prep_kernel(x_ref, halo_ref, sm_ref, cw_ref, al_ref, dtb_ref,
                    q_ref, k_ref, v_ref, b_ref, g_ref, xe_scr, *, first_row, last_row):
    m = pl.program_id(0)
    tm = x_ref.shape[0]
    halo = halo_ref[...].astype(F32)
    xe_scr[0:8, :] = jnp.where(m > 0, halo, jnp.zeros_like(halo))
    xe_scr[8:, :] = x_ref[...].astype(F32)
    cw = cw_ref[...]
    acc = xe_scr[8:, :] * cw[DN_CONV - 1:DN_CONV, :]
    for j in range(DN_CONV - 1):
        acc = acc + xe_scr[pl.ds(8 - (DN_CONV - 1) + j, tm), :] * cw[j:j + 1, :]
    y = _silu(acc)

    rowid = m * tm + lax.broadcasted_iota(jnp.int32, (tm, LANES), 0)
    real = (rowid >= first_row) & (rowid < last_row)
    lane = lax.broadcasted_iota(jnp.int32, (tm, LANES), 1)
    sm = sm_ref[...]
    beta_all = _sigmoid(sm)
    a = sm + dtb_ref[...]
    softplus = jnp.maximum(a, 0.0) + jnp.log(1.0 + jnp.exp(-jnp.abs(a)))
    g_all = -jnp.exp(al_ref[...]) * softplus
    zeros = jnp.zeros((tm, LANES), F32)
    width = N_HEADS * HEAD_W
    for h in range(N_HEADS):
        sl = slice(h * HEAD_W, (h + 1) * HEAD_W)
        qh = y[:, sl]
        kh = y[:, width + h * HEAD_W: width + (h + 1) * HEAD_W]
        vh = y[:, 2 * width + h * HEAD_W: 2 * width + (h + 1) * HEAD_W]
        qh = qh * lax.rsqrt(jnp.sum(qh * qh, axis=-1, keepdims=True) + 1e-6) * (HEAD_W ** -0.5)
        kh = kh * lax.rsqrt(jnp.sum(kh * kh, axis=-1, keepdims=True) + 1e-6)
        q_ref[h] = qh.astype(BF16)
        k_ref[h] = kh.astype(BF16)
        v_ref[h] = vh.astype(BF16)
        bh = jnp.sum(jnp.where(lane == h, beta_all, zeros), axis=-1, keepdims=True)
        gh = jnp.sum(jnp.where(lane == N_HEADS + h, g_all, zeros), axis=-1, keepdims=True)
        b_ref[h] = jnp.where(real, jnp.broadcast_to(bh, (tm, LANES)), zeros)
        g_ref[h] = jnp.where(real, jnp.broadcast_to(gh, (tm, LANES)), zeros)


def _dn_prep(proj, small, conv_w, a_log_row, dtb_row, first_row, last_row):
    lt = proj.shape[0]
    c = 3 * N_HEADS * HEAD_W
    kern = functools.partial(_dn_prep_kernel, first_row=first_row, last_row=last_row)
    hshape = lambda dt: jax.ShapeDtypeStruct((N_HEADS, lt, HEAD_W), dt)
    hspec = pl.BlockSpec((N_HEADS, TILE, HEAD_W), lambda m: (0, m, 0))
    return pl.pallas_call(
        kern,
        grid=(lt // TILE,),
        in_specs=[
            pl.BlockSpec((TILE, c), lambda m: (m, 1)),
            pl.BlockSpec((8, c), lambda m: (jnp.maximum(m * (TILE // 8) - 1, 0), 1)),
            pl.BlockSpec((TILE, LANES), lambda m: (m, 0)),
            pl.BlockSpec((DN_CONV, c), lambda m: (0, 0)),
            pl.BlockSpec((1, LANES), lambda m: (0, 0)),
            pl.BlockSpec((1, LANES), lambda m: (0, 0)),
        ],
        out_specs=[hspec, hspec, hspec, hspec, hspec],
        out_shape=[hshape(BF16), hshape(BF16), hshape(BF16), hshape(F32), hshape(F32)],
        scratch_shapes=[pltpu.VMEM((TILE + 8, c), F32)],
        compiler_params=_cparams(("parallel",)),
        name="dn_prep",
    )(proj, proj, small, conv_w, a_log_row, dtb_row)


def _dn_chunk_kernel(q_ref, k_ref, v_ref, b_ref, g_ref, gate_ref, nw_ref, o_ref, s_scr):
    @pl.when(pl.program_id(1) == 0)
    def _():
        s_scr[...] = jnp.zeros(s_scr.shape, F32)

    r = DN_ROWS
    c = DN_CHUNK
    q = q_ref[0].astype(F32)
    k = k_ref[0].astype(F32)
    v = v_ref[0].astype(F32)
    beta = b_ref[0]
    g = g_ref[0]

    row = lax.broadcasted_iota(jnp.int32, (r, r), 0)
    col = lax.broadcasted_iota(jnp.int32, (r, r), 1)
    same64 = (row // c) == (col // c)
    same32 = (row // 32) == (col // 32)
    same16 = (row // 16) == (col // 16)
    incl = same64 & (row >= col)
    strict = same64 & (row > col)
    zero = jnp.zeros((r, r), F32)
    one = jnp.ones((r, r), F32)
    incl_f = jnp.where(incl, one, zero)
    same_f = jnp.where(same64, one, zero)
    eye = jnp.where(row == col, one, zero)

    g2 = jnp.concatenate([g, g], axis=1)
    diff = _mm_f32(incl_f, jnp.where(strict, g2, zero))
    gc = _mm_f32(incl_f, g)
    gtot = _mm_f32(same_f, g)
    decay = jnp.where(incl, jnp.exp(jnp.where(incl, diff, zero)), zero)
    egc = jnp.exp(gc)
    kb = k * beta
    a = jnp.where(strict, _mm_nt(kb, k) * decay, zero)

    ad = jnp.where(same16, a, zero)
    p2 = _mm(ad, ad)
    p4 = _mm(p2, p2)
    p8 = _mm(p4, p4)
    t = _mm(_mm(_mm(eye - ad, eye + p2), eye + p4), eye + p8)
    a1 = jnp.where(same32 & jnp.logical_not(same16), a, zero)
    t = t - _mm(_mm(t, a1), t)
    a2 = jnp.where(jnp.logical_not(same32), a, zero)
    t = t - _mm(_mm(t, a2), t)

    uw = _mm(t, jnp.concatenate([v * beta, kb * egc], axis=1))
    u = uw[:, :HEAD_W]
    w = uw[:, HEAD_W:]
    qk = jnp.where(incl, _mm_nt(q, k) * decay, zero)
    qd = q * egc
    kd = k * jnp.exp(gtot - gc)
    gts = jnp.exp(gtot)

    s = s_scr[...]
    n_chunks = r // c
    vns = []
    outs = []
    for i in range(n_chunks):
        rs = slice(i * c, (i + 1) * c)
        vn = u[rs] - _mm(w[rs], s)
        vns.append(vn)
        vn_all = jnp.concatenate(vns + [jnp.zeros(((n_chunks - 1 - i) * c, HEAD_W), F32)] * (i < n_chunks - 1),
                                 axis=0)
        outs.append(_mm(qd[rs], s) + _mm(qk[rs], vn_all))
        s = s * gts[i * c:i * c + 1, :] + _mm_tn(kd[rs], vn)
    s_scr[...] = s

    o = jnp.concatenate(outs, axis=0)
    o = o * lax.rsqrt(jnp.mean(o * o, axis=-1, keepdims=True) + NORM_EPS) * nw_ref[...]
    o_ref[...] = (o * _silu(gate_ref[...].astype(F32))).astype(BF16)


def _dn_chunks(qn, kn, vc, beta_b, g_b, proj, norm_w):
    lt = qn.shape[1]
    hspec = pl.BlockSpec((1, DN_ROWS, HEAD_W), lambda h, i: (h, i, 0))
    return pl.pallas_call(
        _dn_chunk_kernel,
        grid=(N_HEADS, lt // DN_ROWS),
        in_specs=[
            hspec, hspec, hspec, hspec, hspec,
            pl.BlockSpec((DN_ROWS, HEAD_W), lambda h, i: (i, CB_DG + h)),
            pl.BlockSpec((1, HEAD_W), lambda h, i: (0, 0)),
        ],
        out_specs=pl.BlockSpec((DN_ROWS, HEAD_W), lambda h, i: (i, h)),
        out_shape=jax.ShapeDtypeStruct((lt, N_HEADS * HEAD_W), BF16),
        scratch_shapes=[pltpu.VMEM((HEAD_W, HEAD_W), F32)],
        compiler_params=_cparams(("parallel", "arbitrary")),
        name="dn_chunks",
    )(qn, kn, vc, beta_b, g_b, proj, norm_w)


def _mix_kernel(ya_ref, yd_ref, ga_ref, gd_ref, h_ref, wa_ref, wd_ref, wo_ref, nw_ref, wr_ref, br_ref,
                h1_ref, u2_ref, lg_ref):
    pa = jnp.dot(ya_ref[...], wa_ref[...], preferred_element_type=F32)
    pd = jnp.dot(yd_ref[...], wd_ref[...], preferred_element_type=F32)
    mixed = _sigmoid(ga_ref[...].astype(F32)) * pa + _sigmoid(gd_ref[...].astype(F32)) * pd
    h1 = h_ref[...] + jnp.dot(mixed.astype(BF16), wo_ref[...], preferred_element_type=F32)
    h1_ref[...] = h1
    u2 = h1 * lax.rsqrt(jnp.mean(h1 * h1, axis=-1, keepdims=True) + NORM_EPS) * nw_ref[...]
    u2_ref[...] = u2.astype(BF16)
    lg_ref[...] = _mm_f32(u2, wr_ref[...]) + br_ref[...]


def _mix(y_a, y_d, proj, hp, w_a, w_d, w_o, norm_w, w_r, b_r):
    lt = hp.shape[0]
    row = lambda: pl.BlockSpec((TILE, D_MODEL), lambda m: (m, 0))
    full = lambda: pl.BlockSpec((D_MODEL, D_MODEL), lambda m: (0, 0))
    return pl.pallas_call(
        _mix_kernel,
        grid=(lt // TILE,),
        in_specs=[
            row(), row(),
            pl.BlockSpec((TILE, D_MODEL), lambda m: (m, CB_GA // 8)),
            pl.BlockSpec((TILE, D_MODEL), lambda m: (m, CB_GD // 8)),
            row(), full(), full(), full(),
            pl.BlockSpec((1, D_MODEL), lambda m: (0, 0)),
            pl.BlockSpec((D_MODEL, LANES), lambda m: (0, 0)),
            pl.BlockSpec((1, LANES), lambda m: (0, 0)),
        ],
        out_specs=[row(), row(), pl.BlockSpec((TILE, LANES), lambda m: (m, 0))],
        out_shape=[
            jax.ShapeDtypeStruct((lt, D_MODEL), F32),
            jax.ShapeDtypeStruct((lt, D_MODEL), BF16),
            jax.ShapeDtypeStruct((lt, LANES), F32),
        ],
        compiler_params=_cparams(("parallel",)),
        name="branch_mix",
    )(y_a, y_d, proj, proj, hp, w_a, w_d, w_o, norm_w, w_r, b_r)


def _route(logits):
    shape = logits.shape
    lane = lax.broadcasted_iota(jnp.int32, shape, 1)
    big = jnp.full(shape, 4 * LANES, jnp.int32)
    neg = jnp.full(shape, NEG, F32)
    zero = jnp.zeros(shape, F32)
    is_g = lane < N_GROUPS
    gl = jnp.where(is_g, logits, neg)
    gmax = jnp.max(gl, axis=-1, keepdims=True)
    gsum = jnp.sum(jnp.where(is_g, jnp.exp(gl - gmax), zero), axis=-1, keepdims=True)
    g_w = 1.0 / gsum
    g_i = jnp.min(jnp.where(is_g & (gl == gmax), lane, big), axis=-1, keepdims=True)
    e_lane = lane - N_GROUPS
    in_grp = (e_lane >= 0) & (e_lane < N_EXPERTS) & ((e_lane // EXPERTS_PER_GROUP) == g_i)
    el = jnp.where(in_grp, logits, neg)
    emax = jnp.max(el, axis=-1, keepdims=True)
    eexp = jnp.where(in_grp, jnp.exp(el - emax), zero)
    ep = eexp / jnp.sum(eexp, axis=-1, keepdims=True)
    minus = jnp.full(shape, -1.0, F32)
    ep1 = jnp.where(in_grp, ep, minus)
    p1 = jnp.max(ep1, axis=-1, keepdims=True)
    i1 = jnp.min(jnp.where(ep1 == p1, lane, big), axis=-1, keepdims=True)
    ep2 = jnp.where(lane == i1, minus, ep1)
    p2 = jnp.max(ep2, axis=-1, keepdims=True)
    i2 = jnp.min(jnp.where(ep2 == p2, lane, big), axis=-1, keepdims=True)
    den = p1 + p2
    return g_w * (jnp.where(lane == i1, p1 / den, zero) + jnp.where(lane == i2, p2 / den, zero))


def _moe_kernel(u_ref, lg_ref, h1_ref, wg_ref, wu_ref, wd_ref, fw_ref, o_ref, c_scr, acc_scr):
    m = pl.program_id(0)
    e = pl.program_id(1)

    @pl.when(e == 0)
    def _():
        c_scr[...] = _route(lg_ref[...])
        acc_scr[...] = jnp.zeros(acc_scr.shape, F32)

    x = u_ref[...]
    a = jnp.dot(x, wg_ref[0], preferred_element_type=F32)
    b = jnp.dot(x, wu_ref[0], preferred_element_type=F32)
    cmb = c_scr[...]
    lane = lax.broadcasted_iota(jnp.int32, cmb.shape, 1)
    ce = jnp.sum(jnp.where(lane == e + N_GROUPS, cmb, jnp.zeros_like(cmb)), axis=-1, keepdims=True)
    hh = _silu(a) * b
    y = jnp.dot(hh.astype(BF16), wd_ref[0], preferred_element_type=F32)
    acc_scr[...] += ce * y

    @pl.when((e == N_EXPERTS - 1) & (m > 0))
    def _():
        hf = h1_ref[...] + acc_scr[...]
        o_ref[...] = hf * lax.rsqrt(jnp.mean(hf * hf, axis=-1, keepdims=True) + NORM_EPS) * fw_ref[...]


def _moe(u2, logits, h1, w_gate, w_up, w_down, final_w, seq):
    lt = u2.shape[0]
    return pl.pallas_call(
        _moe_kernel,
        grid=(lt // TILE, N_EXPERTS),
        in_specs=[
            pl.BlockSpec((TILE, D_MODEL), lambda m, e: (m, 0)),
            pl.BlockSpec((TILE, LANES), lambda m, e: (m, 0)),
            pl.BlockSpec((TILE, D_MODEL), lambda m, e: (m, 0)),
            pl.BlockSpec((1, D_MODEL, D_EXPERT), lambda m, e: (e, 0, 0)),
            pl.BlockSpec((1, D_MODEL, D_EXPERT), lambda m, e: (e, 0, 0)),
            pl.BlockSpec((1, D_EXPERT, D_MODEL), lambda m, e: (e, 0, 0)),
            pl.BlockSpec((1, D_MODEL), lambda m, e: (0, 0)),
        ],
        out_specs=pl.BlockSpec((TILE, D_MODEL), lambda m, e: (jnp.maximum(m - 1, 0), 0)),
        out_shape=jax.ShapeDtypeStruct((seq, D_MODEL), F32),
        scratch_shapes=[pltpu.VMEM((TILE, LANES), F32), pltpu.VMEM((TILE, D_MODEL), F32)],
        compiler_params=_cparams(("arbitrary", "arbitrary")),
        name="moe_final",
    )(u2, logits, h1, w_gate, w_up, w_down, final_w)


def kernel(x, meta_tokens, norm1_w, w_in, lambda_q1, lambda_k1, lambda_q2, lambda_k2, da_subln_w, dn_conv_w,
           dn_A_log, dn_dt_bias, dn_norm_w, w_branch_attn, w_branch_dn, w_out, norm2_w, router_group_w,
           router_group_b, router_expert_w, router_expert_b, expert_w_gate, expert_w_up, expert_w_down,
           final_norm_w):
    assert x.shape[0] == 1 and x.shape[2] == D_MODEL and x.shape[1] % TILE == 0
    assert w_in.shape[0] == 1, "single layer"
    seq = x.shape[1]
    first = TILE - N_META

    hp = jnp.concatenate([jnp.zeros((first, D_MODEL), F32), meta_tokens.astype(F32), x[0]], axis=0)

    w = w_in[0]
    n_main = 7 * D_MODEL
    scale = jnp.concatenate([jnp.full((D_MODEL,), DA_HEAD_DIM ** -0.5, F32),
                             jnp.ones((PROJ_COLS - D_MODEL,), F32)])
    w_big = (jnp.concatenate([w[:, :n_main], w[:, n_main + 2 * N_HEADS:]], axis=1) * scale).astype(BF16)
    w_small = jnp.pad(w[:, n_main:n_main + 2 * N_HEADS], ((0, 0), (0, LANES - 2 * N_HEADS)))
    proj, small = _inproj(hp, norm1_w, w_big, w_small)

    slopes = jnp.broadcast_to(jnp.asarray(ALIBI_SLOPES, F32)[:, None], (N_HEADS, LANES))
    lam_tab = jnp.pad(jnp.concatenate([lambda_q1, lambda_k1, lambda_q2, lambda_k2], axis=0),
                      ((0, 4), (0, LANES - DA_HEAD_DIM)))
    y_a = _attention(proj, slopes, lam_tab, da_subln_w, first)

    pad16 = lambda t, off: jnp.pad(t, ((0, 0), (off, LANES - N_HEADS - off)))
    qn, kn, vc, beta_b, g_b = _dn_prep(proj, small, dn_conv_w[0], pad16(dn_A_log, N_HEADS),
                                       pad16(dn_dt_bias, N_HEADS), first, TILE + seq)
    y_d = _dn_chunks(qn, kn, vc, beta_b, g_b, proj, dn_norm_w)

    w_r = jnp.pad(jnp.concatenate([router_group_w[0], router_expert_w[0]], axis=1),
                  ((0, 0), (0, LANES - N_GROUPS - N_EXPERTS)))
    b_r = jnp.pad(jnp.concatenate([router_group_b, router_expert_b], axis=1),
                  ((0, 0), (0, LANES - N_GROUPS - N_EXPERTS)))
    h1, u2, logits = _mix(y_a, y_d, proj, hp, w_branch_attn[0].astype(BF16), w_branch_dn[0].astype(BF16),
                          w_out[0].astype(BF16), norm2_w, w_r, b_r)

    out = _moe(u2, logits, h1, expert_w_gate[0].astype(BF16), expert_w_up[0].astype(BF16),
               expert_w_down[0].astype(BF16), final_norm_w[None, :], seq)
    return out[None]
```

```python
import functools
import math

import jax
import jax.numpy as jnp
from jax import lax
from jax.experimental import pallas as pl
from jax.experimental.pallas import tpu as pltpu

F32 = jnp.float32
BF16 = jnp.bfloat16
HIGHEST = lax.Precision.HIGHEST

D_MODEL = 1024
N_META = 16
NORM_EPS = 1e-6
N_HEADS = 8
HEAD_W = 128
DA_HEAD_DIM = 64
DN_CONV = 4
DN_CHUNK = 64
N_GROUPS = 4
EXPERTS_PER_GROUP = 8
N_EXPERTS = 32
D_EXPERT = 256
LAMBDA_INIT = 0.8 - 0.6 * math.exp(-0.3 * 0)
ALIBI_SLOPES = tuple(2.0 ** (-8.0 * (h + 1) / N_HEADS) for h in range(N_HEADS))

TILE = 512
DN_ROWS = 256
LANES = 128
NEG = -1e30
VMEM_LIMIT = 56 * 1024 * 1024

CB_AQ, CB_AK, CB_AV, CB_DQ, CB_DK, CB_DV, CB_DG, CB_GA, CB_GD = 0, 8, 16, 24, 32, 40, 48, 56, 64
PROJ_COLS = 72 * 128


def _cparams(sem):
    return pltpu.CompilerParams(dimension_semantics=sem, vmem_limit_bytes=VMEM_LIMIT)


def _mm(a, b):
    return jnp.dot(a.astype(BF16), b.astype(BF16), preferred_element_type=F32)


def _mm_nt(a, b):
    return lax.dot_general(a.astype(BF16), b.astype(BF16), (((1,), (1,)), ((), ())),
                           preferred_element_type=F32)


def _mm_tn(a, b):
    return lax.dot_general(a.astype(BF16), b.astype(BF16), (((0,), (0,)), ((), ())),
                           preferred_element_type=F32)


def _mm_f32(a, b):
    return jnp.dot(a, b, precision=HIGHEST, preferred_element_type=F32)


def _silu(x):
    return x * (1.0 / (1.0 + jnp.exp(-x)))


def _sigmoid(x):
    return 1.0 / (1.0 + jnp.exp(-x))


def _inproj_kernel(x_ref, nw_ref, w_ref, ws_ref, o_ref, os_ref, u_scr):
    @pl.when(pl.program_id(1) == 0)
    def _():
        x = x_ref[...]
        u = x * lax.rsqrt(jnp.mean(x * x, axis=-1, keepdims=True) + NORM_EPS) * nw_ref[...]
        u_scr[...] = u.astype(BF16)
        os_ref[...] = _mm_f32(u, ws_ref[...])

    o_ref[...] = jnp.dot(u_scr[...], w_ref[...], preferred_element_type=F32).astype(BF16)


def _inproj(hp, norm_w, w_big, w_small):
    lt = hp.shape[0]
    tn = 1024
    return pl.pallas_call(
        _inproj_kernel,
        grid=(lt // TILE, PROJ_COLS // tn),
        in_specs=[
            pl.BlockSpec((TILE, D_MODEL), lambda m, n: (m, 0)),
            pl.BlockSpec((1, D_MODEL), lambda m, n: (0, 0)),
            pl.BlockSpec((D_MODEL, tn), lambda m, n: (0, n)),
            pl.BlockSpec((D_MODEL, LANES), lambda m, n: (0, 0)),
        ],
        out_specs=[
            pl.BlockSpec((TILE, tn), lambda m, n: (m, n)),
            pl.BlockSpec((TILE, LANES), lambda m, n: (m, 0)),
        ],
        out_shape=[
            jax.ShapeDtypeStruct((lt, PROJ_COLS), BF16),
            jax.ShapeDtypeStruct((lt, LANES), F32),
        ],
        scratch_shapes=[pltpu.VMEM((TILE, D_MODEL), BF16)],
        compiler_params=_cparams(("parallel", "arbitrary")),
        name="inproj",
    )(hp, norm_w, w_big, w_small)


V_ROWS = 144
Q_HALF = 256


def _attn_kernel(qt_ref, k_ref, vt_ref, slope_ref, lam_ref, sw_ref, o_ref,
                 m_ref, a_ref, s_ref, *, first_key):
    h = pl.program_id(0)
    qi = pl.program_id(1)
    tq = qt_ref.shape[-1]
    tk = k_ref.shape[2]

    qt = qt_ref[0]
    slope = slope_ref[pl.ds(h, 1), :][:, :1]
    rowi = lax.broadcasted_iota(jnp.int32, (HEAD_W, tq), 0)
    qbase = -(slope * (qi * tq).astype(F32))
    brow = jnp.where(rowi < 2, 1.0, jnp.where(rowi == 2, qbase, 0.0)).astype(BF16)
    z64 = jnp.zeros((DA_HEAD_DIM, tq), BF16)
    w1 = jnp.concatenate([qt[:DA_HEAD_DIM], z64, brow], axis=0)
    w2 = jnp.concatenate([z64, qt[DA_HEAD_DIM:], brow], axis=0)

    m_ref[...] = jnp.full(m_ref.shape, NEG, F32)
    a_ref[...] = jnp.zeros(a_ref.shape, F32)
    n_half = tq // Q_HALF
    chains = [(w, hf) for w in (w1, w2) for hf in range(n_half)]

    def tile(ki, masked):
        kt = k_ref[0, ki]
        vt = vt_ref[0, ki]
        if masked:
            key = ki * tk + lax.broadcasted_iota(jnp.int32, (tk, Q_HALF), 0)
            qry = qi * tq + lax.broadcasted_iota(jnp.int32, (tk, Q_HALF), 1)
        tile_max = []
        for c, (w, hf) in enumerate(chains):
            st = jnp.dot(kt, w[:, hf * Q_HALF:(hf + 1) * Q_HALF], preferred_element_type=F32)
            if masked:
                valid = (key <= qry + hf * Q_HALF) & (key >= first_key)
                st = jnp.where(valid, st, NEG)
            s_ref[c] = st
            tile_max.append(jnp.max(st, axis=0, keepdims=True))
        for c in range(len(chains)):
            m_old = m_ref[c]
            m_new = jnp.maximum(m_old, tile_max[c])
            alpha = jnp.exp(m_old - m_new)
            p = jnp.exp(s_ref[c] - m_new).astype(BF16)
            a_ref[c] = alpha * a_ref[c] + jnp.dot(vt, p, preferred_element_type=F32)
            m_ref[c] = m_new

    tile(0, True)

    def body(ki, carry):
        tile(ki, False)
        return carry

    lax.fori_loop(1, qi, body, 0)

    @pl.when(qi > 0)
    def _():
        tile(qi, True)

    lt = lam_ref[...]
    lam = (jnp.exp(jnp.sum(lt[0:1] * lt[1:2], axis=-1, keepdims=True))
           - jnp.exp(jnp.sum(lt[2:3] * lt[3:4], axis=-1, keepdims=True)) + LAMBDA_INIT)
    a1 = jnp.concatenate([a_ref[c] for c in range(n_half)], axis=1)
    a2 = jnp.concatenate([a_ref[n_half + c] for c in range(n_half)], axis=1)
    ot = a1[:HEAD_W] / a1[HEAD_W:HEAD_W + 1] - lam * (a2[:HEAD_W] / a2[HEAD_W:HEAD_W + 1])
    o = ot.T
    o = o * lax.rsqrt(jnp.mean(o * o, axis=-1, keepdims=True) + NORM_EPS) * sw_ref[...]
    o_ref[...] = (o * (1.0 - LAMBDA_INIT)).astype(BF16)


def _attention(qt, k_aug, vt_aug, slopes, lam_tab, subln_w, first_key):
    n_k, tk = k_aug.shape[1], k_aug.shape[2]
    lt = n_k * tk
    tq = TILE
    kern = functools.partial(_attn_kernel, first_key=first_key)
    n_chain = 2 * tq // Q_HALF
    scratch = [pltpu.VMEM((n_chain, 1, Q_HALF), F32), pltpu.VMEM((n_chain, V_ROWS, Q_HALF), F32),
               pltpu.VMEM((n_chain, tk, Q_HALF), F32)]
    return pl.pallas_call(
        kern,
        grid=(N_HEADS, lt // tq),
        in_specs=[
            pl.BlockSpec((1, HEAD_W, tq), lambda h, i: (h, 0, i)),
            pl.BlockSpec((1, n_k, tk, 2 * HEAD_W), lambda h, i: (h, 0, 0, 0)),
            pl.BlockSpec((1, n_k, V_ROWS, tk), lambda h, i: (h, 0, 0, 0)),
            pl.BlockSpec((8, LANES), lambda h, i: (0, 0)),
            pl.BlockSpec((8, LANES), lambda h, i: (0, 0)),
            pl.BlockSpec((1, HEAD_W), lambda h, i: (0, 0)),
        ],
        out_specs=pl.BlockSpec((tq, HEAD_W), lambda h, i: (i, h)),
        out_shape=jax.ShapeDtypeStruct((lt, N_HEADS * HEAD_W), BF16),
        scratch_shapes=scratch,
        compiler_params=_cparams(("parallel", "arbitrary")),
        name="diff_attn",
    )(qt, k_aug, vt_aug, slopes, lam_tab, subln_w)


def _attn_operands(proj):
    lt = proj.shape[0]
    n_k = lt // TILE
    heads = lambda cb: proj[:, cb * LANES:(cb + N_HEADS) * LANES].reshape(lt, N_HEADS, HEAD_W)
    qt = heads(CB_AQ).transpose(1, 2, 0)
    pos = jnp.arange(lt, dtype=jnp.int32)
    slope = jnp.asarray(ALIBI_SLOPES, F32)[:, None]
    hi = ((pos >> 7) << 7).astype(F32)[None, :] * slope
    lo = (pos & 127).astype(F32)[None, :] * slope
    cols = jnp.stack([hi, lo, jnp.ones_like(hi)], axis=-1).astype(BF16)
    k_aug = jnp.concatenate([heads(CB_AK).transpose(1, 0, 2), cols,
                             jnp.zeros((N_HEADS, lt, HEAD_W - 3), BF16)], axis=-1)
    k_aug = k_aug.reshape(N_HEADS, n_k, TILE, 2 * HEAD_W)
    vt = heads(CB_AV).reshape(n_k, TILE, N_HEADS, HEAD_W).transpose(2, 0, 3, 1)
    vt_aug = jnp.concatenate([vt, jnp.ones((N_HEADS, n_k, 1, TILE), BF16),
                              jnp.zeros((N_HEADS, n_k, V_ROWS - HEAD_W - 1, TILE), BF16)], axis=2)
    return qt, k_aug, vt_aug


def _dn_prep_kernel(x_ref, halo_ref, sm_ref, cw_ref, al_ref, dtb_ref,
                    q_ref, k_ref, v_ref, b_ref, g_ref, xe_scr, *, first_row, last_row):
    m = pl.program_id(0)
    tm = x_ref.shape[0]
    halo = halo_ref[...].astype(F32)
    xe_scr[0:8, :] = jnp.where(m > 0, halo, jnp.zeros_like(halo))
    xe_scr[8:, :] = x_ref[...].astype(F32)
    cw = cw_ref[...]
    acc = xe_scr[8:, :] * cw[DN_CONV - 1:DN_CONV, :]
    for j in range(DN_CONV - 1):
        acc = acc + xe_scr[pl.ds(8 - (DN_CONV - 1) + j, tm), :] * cw[j:j + 1, :]
    y = _silu(acc)

    rowid = m * tm + lax.broadcasted_iota(jnp.int32, (tm, LANES), 0)
    real = (rowid >= first_row) & (rowid < last_row)
    lane = lax.broadcasted_iota(jnp.int32, (tm, LANES), 1)
    sm = sm_ref[...]
    beta_all = _sigmoid(sm)
    a = sm + dtb_ref[...]
    softplus = jnp.maximum(a, 0.0) + jnp.log(1.0 + jnp.exp(-jnp.abs(a)))
    g_all = -jnp.exp(al_ref[...]) * softplus
    zeros = jnp.zeros((tm, LANES), F32)
    width = N_HEADS * HEAD_W
    for h in range(N_HEADS):
        sl = slice(h * HEAD_W, (h + 1) * HEAD_W)
        qh = y[:, sl]
        kh = y[:, width + h * HEAD_W: width + (h + 1) * HEAD_W]
        vh = y[:, 2 * width + h * HEAD_W: 2 * width + (h + 1) * HEAD_W]
        qh = qh * lax.rsqrt(jnp.sum(qh * qh, axis=-1, keepdims=True) + 1e-6) * (HEAD_W ** -0.5)
        kh = kh * lax.rsqrt(jnp.sum(kh * kh, axis=-1, keepdims=True) + 1e-6)
        q_ref[h] = qh.astype(BF16)
        k_ref[h] = kh.astype(BF16)
        v_ref[h] = vh.astype(BF16)
        bh = jnp.sum(jnp.where(lane == h, beta_all, zeros), axis=-1, keepdims=True)
        gh = jnp.sum(jnp.where(lane == N_HEADS + h, g_all, zeros), axis=-1, keepdims=True)
        b_ref[h] = jnp.where(real, jnp.broadcast_to(bh, (tm, LANES)), zeros)
        g_ref[h] = jnp.where(real, jnp.broadcast_to(gh, (tm, LANES)), zeros)


def _dn_prep(proj, small, conv_w, a_log_row, dtb_row, first_row, last_row):
    lt = proj.shape[0]
    c = 3 * N_HEADS * HEAD_W
    kern = functools.partial(_dn_prep_kernel, first_row=first_row, last_row=last_row)
    hshape = lambda dt: jax.ShapeDtypeStruct((N_HEADS, lt, HEAD_W), dt)
    hspec = pl.BlockSpec((N_HEADS, TILE, HEAD_W), lambda m: (0, m, 0))
    return pl.pallas_call(
        kern,
        grid=(lt // TILE,),
        in_specs=[
            pl.BlockSpec((TILE, c), lambda m: (m, 1)),
            pl.BlockSpec((8, c), lambda m: (jnp.maximum(m * (TILE // 8) - 1, 0), 1)),
            pl.BlockSpec((TILE, LANES), lambda m: (m, 0)),
            pl.BlockSpec((DN_CONV, c), lambda m: (0, 0)),
            pl.BlockSpec((1, LANES), lambda m: (0, 0)),
            pl.BlockSpec((1, LANES), lambda m: (0, 0)),
        ],
        out_specs=[hspec, hspec, hspec, hspec, hspec],
        out_shape=[hshape(BF16), hshape(BF16), hshape(BF16), hshape(F32), hshape(F32)],
        scratch_shapes=[pltpu.VMEM((TILE + 8, c), F32)],
        compiler_params=_cparams(("parallel",)),
        name="dn_prep",
    )(proj, proj, small, conv_w, a_log_row, dtb_row)


def _dn_chunk_kernel(q_ref, k_ref, v_ref, b_ref, g_ref, gate_ref, nw_ref, o_ref, s_scr):
    @pl.when(pl.program_id(1) == 0)
    def _():
        s_scr[...] = jnp.zeros(s_scr.shape, F32)

    r = DN_ROWS
    c = DN_CHUNK
    q = q_ref[0].astype(F32)
    k = k_ref[0].astype(F32)
    v = v_ref[0].astype(F32)
    beta = b_ref[0]
    g = g_ref[0]

    row = lax.broadcasted_iota(jnp.int32, (r, r), 0)
    col = lax.broadcasted_iota(jnp.int32, (r, r), 1)
    same64 = (row // c) == (col // c)
    same32 = (row // 32) == (col // 32)
    same16 = (row // 16) == (col // 16)
    incl = same64 & (row >= col)
    strict = same64 & (row > col)
    zero = jnp.zeros((r, r), F32)
    one = jnp.ones((r, r), F32)
    incl_f = jnp.where(incl, one, zero)
    same_f = jnp.where(same64, one, zero)
    eye = jnp.where(row == col, one, zero)

    g2 = jnp.concatenate([g, g], axis=1)
    diff = _mm_f32(incl_f, jnp.where(strict, g2, zero))
    gc = _mm_f32(incl_f, g)
    gtot = _mm_f32(same_f, g)
    decay = jnp.where(incl, jnp.exp(jnp.where(incl, diff, zero)), zero)
    egc = jnp.exp(gc)
    kb = k * beta
    a = jnp.where(strict, _mm_nt(kb, k) * decay, zero)

    ad = jnp.where(same16, a, zero)
    p2 = _mm(ad, ad)
    p4 = _mm(p2, p2)
    p8 = _mm(p4, p4)
    t = _mm(_mm(_mm(eye - ad, eye + p2), eye + p4), eye + p8)
    a1 = jnp.where(same32 & jnp.logical_not(same16), a, zero)
    t = t - _mm(_mm(t, a1), t)
    a2 = jnp.where(jnp.logical_not(same32), a, zero)
    t = t - _mm(_mm(t, a2), t)

    uw = _mm(t, jnp.concatenate([v * beta, kb * egc], axis=1))
    u = uw[:, :HEAD_W]
    w = uw[:, HEAD_W:]
    qk = jnp.where(incl, _mm_nt(q, k) * decay, zero)
    qd = q * egc
    kd = k * jnp.exp(gtot - gc)
    gts = jnp.exp(gtot)

    s = s_scr[...]
    n_chunks = r // c
    vns = []
    outs = []
    for i in range(n_chunks):
        rs = slice(i * c, (i + 1) * c)
        vn = u[rs] - _mm(w[rs], s)
        vns.append(vn)
        vn_all = jnp.concatenate(vns + [jnp.zeros(((n_chunks - 1 - i) * c, HEAD_W), F32)] * (i < n_chunks - 1),
                                 axis=0)
        outs.append(_mm(qd[rs], s) + _mm(qk[rs], vn_all))
        s = s * gts[i * c:i * c + 1, :] + _mm_tn(kd[rs], vn)
    s_scr[...] = s

    o = jnp.concatenate(outs, axis=0)
    o = o * lax.rsqrt(jnp.mean(o * o, axis=-1, keepdims=True) + NORM_EPS) * nw_ref[...]
    o_ref[...] = (o * _silu(gate_ref[...].astype(F32))).astype(BF16)


def _dn_chunks(qn, kn, vc, beta_b, g_b, proj, norm_w):
    lt = qn.shape[1]
    hspec = pl.BlockSpec((1, DN_ROWS, HEAD_W), lambda h, i: (h, i, 0))
    return pl.pallas_call(
        _dn_chunk_kernel,
        grid=(N_HEADS, lt // DN_ROWS),
        in_specs=[
            hspec, hspec, hspec, hspec, hspec,
            pl.BlockSpec((DN_ROWS, HEAD_W), lambda h, i: (i, CB_DG + h)),
            pl.BlockSpec((1, HEAD_W), lambda h, i: (0, 0)),
        ],
        out_specs=pl.BlockSpec((DN_ROWS, HEAD_W), lambda h, i: (i, h)),
        out_shape=jax.ShapeDtypeStruct((lt, N_HEADS * HEAD_W), BF16),
        scratch_shapes=[pltpu.VMEM((HEAD_W, HEAD_W), F32)],
        compiler_params=_cparams(("parallel", "arbitrary")),
        name="dn_chunks",
    )(qn, kn, vc, beta_b, g_b, proj, norm_w)


def _mix_kernel(ya_ref, yd_ref, ga_ref, gd_ref, h_ref, wa_ref, wd_ref, wo_ref, nw_ref, wr_ref, br_ref,
                h1_ref, u2_ref, lg_ref):
    pa = jnp.dot(ya_ref[...], wa_ref[...], preferred_element_type=F32)
    pd = jnp.dot(yd_ref[...], wd_ref[...], preferred_element_type=F32)
    mixed = _sigmoid(ga_ref[...].astype(F32)) * pa + _sigmoid(gd_ref[...].astype(F32)) * pd
    h1 = h_ref[...] + jnp.dot(mixed.astype(BF16), wo_ref[...], preferred_element_type=F32)
    h1_ref[...] = h1
    u2 = h1 * lax.rsqrt(jnp.mean(h1 * h1, axis=-1, keepdims=True) + NORM_EPS) * nw_ref[...]
    u2_ref[...] = u2.astype(BF16)
    lg_ref[...] = _mm_f32(u2, wr_ref[...]) + br_ref[...]


def _mix(y_a, y_d, proj, hp, w_a, w_d, w_o, norm_w, w_r, b_r):
    lt = hp.shape[0]
    row = lambda: pl.BlockSpec((TILE, D_MODEL), lambda m: (m, 0))
    full = lambda: pl.BlockSpec((D_MODEL, D_MODEL), lambda m: (0, 0))
    return pl.pallas_call(
        _mix_kernel,
        grid=(lt // TILE,),
        in_specs=[
            row(), row(),
            pl.BlockSpec((TILE, D_MODEL), lambda m: (m, CB_GA // 8)),
            pl.BlockSpec((TILE, D_MODEL), lambda m: (m, CB_GD // 8)),
            row(), full(), full(), full(),
            pl.BlockSpec((1, D_MODEL), lambda m: (0, 0)),
            pl.BlockSpec((D_MODEL, LANES), lambda m: (0, 0)),
            pl.BlockSpec((1, LANES), lambda m: (0, 0)),
        ],
        out_specs=[row(), row(), pl.BlockSpec((TILE, LANES), lambda m: (m, 0))],
        out_shape=[
            jax.ShapeDtypeStruct((lt, D_MODEL), F32),
            jax.ShapeDtypeStruct((lt, D_MODEL), BF16),
            jax.ShapeDtypeStruct((lt, LANES), F32),
        ],
        compiler_params=_cparams(("parallel",)),
        name="branch_mix",
    )(y_a, y_d, proj, proj, hp, w_a, w_d, w_o, norm_w, w_r, b_r)


def _route(logits):
    shape = logits.shape
    lane = lax.broadcasted_iota(jnp.int32, shape, 1)
    big = jnp.full(shape, 4 * LANES, jnp.int32)
    neg = jnp.full(shape, NEG, F32)
    zero = jnp.zeros(shape, F32)
    is_g = lane < N_GROUPS
    gl = jnp.where(is_g, logits, neg)
    gmax = jnp.max(gl, axis=-1, keepdims=True)
    gsum = jnp.sum(jnp.where(is_g, jnp.exp(gl - gmax), zero), axis=-1, keepdims=True)
    g_w = 1.0 / gsum
    g_i = jnp.min(jnp.where(is_g & (gl == gmax), lane, big), axis=-1, keepdims=True)
    e_lane = lane - N_GROUPS
    in_grp = (e_lane >= 0) & (e_lane < N_EXPERTS) & ((e_lane // EXPERTS_PER_GROUP) == g_i)
    el = jnp.where(in_grp, logits, neg)
    emax = jnp.max(el, axis=-1, keepdims=True)
    eexp = jnp.where(in_grp, jnp.exp(el - emax), zero)
    ep = eexp / jnp.sum(eexp, axis=-1, keepdims=True)
    minus = jnp.full(shape, -1.0, F32)
    ep1 = jnp.where(in_grp, ep, minus)
    p1 = jnp.max(ep1, axis=-1, keepdims=True)
    i1 = jnp.min(jnp.where(ep1 == p1, lane, big), axis=-1, keepdims=True)
    ep2 = jnp.where(lane == i1, minus, ep1)
    p2 = jnp.max(ep2, axis=-1, keepdims=True)
    i2 = jnp.min(jnp.where(ep2 == p2, lane, big), axis=-1, keepdims=True)
    den = p1 + p2
    return g_w * (jnp.where(lane == i1, p1 / den, zero) + jnp.where(lane == i2, p2 / den, zero))


def _moe_kernel(u_ref, lg_ref, h1_ref, wg_ref, wu_ref, wd_ref, fw_ref, o_ref, c_scr, acc_scr):
    m = pl.program_id(0)
    e = pl.program_id(1)

    @pl.when(e == 0)
    def _():
        c_scr[...] = _route(lg_ref[...])
        acc_scr[...] = jnp.zeros(acc_scr.shape, F32)

    x = u_ref[...]
    a = jnp.dot(x, wg_ref[0], preferred_element_type=F32)
    b = jnp.dot(x, wu_ref[0], preferred_element_type=F32)
    cmb = c_scr[...]
    lane = lax.broadcasted_iota(jnp.int32, cmb.shape, 1)
    ce = jnp.sum(jnp.where(lane == e + N_GROUPS, cmb, jnp.zeros_like(cmb)), axis=-1, keepdims=True)
    hh = _silu(a) * b
    y = jnp.dot(hh.astype(BF16), wd_ref[0], preferred_element_type=F32)
    acc_scr[...] += ce * y

    @pl.when((e == N_EXPERTS - 1) & (m > 0))
    def _():
        hf = h1_ref[...] + acc_scr[...]
        o_ref[...] = hf * lax.rsqrt(jnp.mean(hf * hf, axis=-1, keepdims=True) + NORM_EPS) * fw_ref[...]


def _moe(u2, logits, h1, w_gate, w_up, w_down, final_w, seq):
    lt = u2.shape[0]
    return pl.pallas_call(
        _moe_kernel,
        grid=(lt // TILE, N_EXPERTS),
        in_specs=[
            pl.BlockSpec((TILE, D_MODEL), lambda m, e: (m, 0)),
            pl.BlockSpec((TILE, LANES), lambda m, e: (m, 0)),
            pl.BlockSpec((TILE, D_MODEL), lambda m, e: (m, 0)),
            pl.BlockSpec((1, D_MODEL, D_EXPERT), lambda m, e: (e, 0, 0)),
            pl.BlockSpec((1, D_MODEL, D_EXPERT), lambda m, e: (e, 0, 0)),
            pl.BlockSpec((1, D_EXPERT, D_MODEL), lambda m, e: (e, 0, 0)),
            pl.BlockSpec((1, D_MODEL), lambda m, e: (0, 0)),
        ],
        out_specs=pl.BlockSpec((TILE, D_MODEL), lambda m, e: (jnp.maximum(m - 1, 0), 0)),
        out_shape=jax.ShapeDtypeStruct((seq, D_MODEL), F32),
        scratch_shapes=[pltpu.VMEM((TILE, LANES), F32), pltpu.VMEM((TILE, D_MODEL), F32)],
        compiler_params=_cparams(("arbitrary", "arbitrary")),
        name="moe_final",
    )(u2, logits, h1, w_gate, w_up, w_down, final_w)


def kernel(x, meta_tokens, norm1_w, w_in, lambda_q1, lambda_k1, lambda_q2, lambda_k2, da_subln_w, dn_conv_w,
           dn_A_log, dn_dt_bias, dn_norm_w, w_branch_attn, w_branch_dn, w_out, norm2_w, router_group_w,
           router_group_b, router_expert_w, router_expert_b, expert_w_gate, expert_w_up, expert_w_down,
           final_norm_w):
    assert x.shape[0] == 1 and x.shape[2] == D_MODEL and x.shape[1] % TILE == 0
    assert w_in.shape[0] == 1, "single layer"
    seq = x.shape[1]
    first = TILE - N_META

    hp = jnp.concatenate([jnp.zeros((first, D_MODEL), F32), meta_tokens.astype(F32), x[0]], axis=0)

    w = w_in[0]
    n_main = 7 * D_MODEL
    scale = jnp.concatenate([jnp.full((D_MODEL,), DA_HEAD_DIM ** -0.5, F32),
                             jnp.ones((PROJ_COLS - D_MODEL,), F32)])
    w_big = (jnp.concatenate([w[:, :n_main], w[:, n_main + 2 * N_HEADS:]], axis=1) * scale).astype(BF16)
    w_small = jnp.pad(w[:, n_main:n_main + 2 * N_HEADS], ((0, 0), (0, LANES - 2 * N_HEADS)))
    proj, small = _inproj(hp, norm1_w, w_big, w_small)

    slopes = jnp.broadcast_to(jnp.asarray(ALIBI_SLOPES, F32)[:, None], (N_HEADS, LANES))
    lam_tab = jnp.pad(jnp.concatenate([lambda_q1, lambda_k1, lambda_q2, lambda_k2], axis=0),
                      ((0, 4), (0, LANES - DA_HEAD_DIM)))
    qt, k_aug, vt_aug = _attn_operands(proj)
    y_a = _attention(qt, k_aug, vt_aug, slopes, lam_tab, da_subln_w, first)

    pad16 = lambda t, off: jnp.pad(t, ((0, 0), (off, LANES - N_HEADS - off)))
    qn, kn, vc, beta_b, g_b = _dn_prep(proj, small, dn_conv_w[0], pad16(dn_A_log, N_HEADS),
                                       pad16(dn_dt_bias, N_HEADS), first, TILE + seq)
    y_d = _dn_chunks(qn, kn, vc, beta_b, g_b, proj, dn_norm_w)

    w_r = jnp.pad(jnp.concatenate([router_group_w[0], router_expert_w[0]], axis=1),
                  ((0, 0), (0, LANES - N_GROUPS - N_EXPERTS)))
    b_r = jnp.pad(jnp.concatenate([router_group_b, router_expert_b], axis=1),
                  ((0, 0), (0, LANES - N_GROUPS - N_EXPERTS)))
    h1, u2, logits = _mix(y_a, y_d, proj, hp, w_branch_attn[0].astype(BF16), w_branch_dn[0].astype(BF16),
                          w_out[0].astype(BF16), norm2_w, w_r, b_r)

    out = _moe(u2, logits, h1, expert_w_gate[0].astype(BF16), expert_w_up[0].astype(BF16),
               expert_w_down[0].astype(BF16), final_norm_w[None, :], seq)
    return out[None]
```

```python
import functools
import math

import jax
import jax.numpy as jnp
from jax import lax
from jax.experimental import pallas as pl
from jax.experimental.pallas import tpu as pltpu

F32 = jnp.float32
BF16 = jnp.bfloat16
HIGHEST = lax.Precision.HIGHEST

D_MODEL = 1024
N_META = 16
NORM_EPS = 1e-6
N_HEADS = 8
HEAD_W = 128
DA_HEAD_DIM = 64
DN_CONV = 4
DN_CHUNK = 64
N_GROUPS = 4
EXPERTS_PER_GROUP = 8
N_EXPERTS = 32
D_EXPERT = 256
LAMBDA_INIT = 0.8 - 0.6 * math.exp(-0.3 * 0)
ALIBI_SLOPES = tuple(2.0 ** (-8.0 * (h + 1) / N_HEADS) for h in range(N_HEADS))

TILE = 512
ATT_TQ = 1024
Q_CHAIN = 256
V_ROWS = 144
DN_ROWS = 256
DN_HEADS_PER_STEP = 4
LANES = 128
NEG = -1e30
VMEM_LIMIT = 56 * 1024 * 1024

CB_DQKV, CB_AK, CB_DG, CB_GA, CB_GD = 0, 24, 32, 40, 48
PROJ_COLS = 56 * 128
PROJ_TN = PROJ_COLS // 2


def _cparams(sem):
    return pltpu.CompilerParams(dimension_semantics=sem, vmem_limit_bytes=VMEM_LIMIT)


def _mm(a, b):
    return jnp.dot(a.astype(BF16), b.astype(BF16), preferred_element_type=F32)


def _mm_nt(a, b):
    return lax.dot_general(a.astype(BF16), b.astype(BF16), (((1,), (1,)), ((), ())),
                           preferred_element_type=F32)


def _mm_tn(a, b):
    return lax.dot_general(a.astype(BF16), b.astype(BF16), (((0,), (0,)), ((), ())),
                           preferred_element_type=F32)


def _mm_f32(a, b):
    return jnp.dot(a, b, precision=HIGHEST, preferred_element_type=F32)


def _silu(x):
    return x * (1.0 / (1.0 + jnp.exp(-x)))


def _sigmoid(x):
    return 1.0 / (1.0 + jnp.exp(-x))


def _inproj_kernel(x_ref, nw_ref, w_ref, wt_ref, ws_ref, o_ref, ot_ref, os_ref, u_scr):
    @pl.when(pl.program_id(1) == 0)
    def _():
        x = x_ref[...]
        u = x * lax.rsqrt(jnp.mean(x * x, axis=-1, keepdims=True) + NORM_EPS) * nw_ref[...]
        ub = u.astype(BF16)
        u_scr[...] = ub
        os_ref[...] = _mm_f32(u, ws_ref[...])
        ot_ref[0] = lax.dot_general(wt_ref[...], ub, (((1,), (1,)), ((), ())),
                                    preferred_element_type=F32).astype(BF16)

    o_ref[...] = jnp.dot(u_scr[...], w_ref[...], preferred_element_type=F32).astype(BF16)


def _inproj(hp, norm_w, w_big, w_qv_t, w_small):
    lt = hp.shape[0]
    n_t = lt // TILE
    return pl.pallas_call(
        _inproj_kernel,
        grid=(n_t, PROJ_COLS // PROJ_TN),
        in_specs=[
            pl.BlockSpec((TILE, D_MODEL), lambda m, n: (m, 0)),
            pl.BlockSpec((1, D_MODEL), lambda m, n: (0, 0)),
            pl.BlockSpec((D_MODEL, PROJ_TN), lambda m, n: (0, n)),
            pl.BlockSpec((2 * D_MODEL, D_MODEL), lambda m, n: (0, 0)),
            pl.BlockSpec((D_MODEL, LANES), lambda m, n: (0, 0)),
        ],
        out_specs=[
            pl.BlockSpec((TILE, PROJ_TN), lambda m, n: (m, n)),
            pl.BlockSpec((1, 2 * D_MODEL, TILE), lambda m, n: (m, 0, 0)),
            pl.BlockSpec((TILE, LANES), lambda m, n: (m, 0)),
        ],
        out_shape=[
            jax.ShapeDtypeStruct((lt, PROJ_COLS), BF16),
            jax.ShapeDtypeStruct((n_t, 2 * D_MODEL, TILE), BF16),
            jax.ShapeDtypeStruct((lt, LANES), F32),
        ],
        scratch_shapes=[pltpu.VMEM((TILE, D_MODEL), BF16)],
        compiler_params=_cparams(("parallel", "arbitrary")),
        name="inproj",
    )(hp, norm_w, w_big, w_qv_t, w_small)


def _attn_kernel(q0_ref, q1_ref, k_ref, kpos_ref, vt_ref, slope_ref, lam_ref, sw_ref, o_ref,
                 m_ref, a_ref, s_ref, *, first_key):
    h = pl.program_id(0)
    j = pl.program_id(1)
    tk = TILE
    tq = ATT_TQ
    n_q = tq // Q_CHAIN

    qt = jnp.concatenate([q0_ref[0], q1_ref[0]], axis=1)
    slope = slope_ref[pl.ds(h, 1), :][:, :1]
    qbase = TILE + j * tq
    rowi = lax.broadcasted_iota(jnp.int32, (HEAD_W, tq), 0)
    brow = jnp.where(rowi < 2, slope, jnp.where(rowi == 2, -(slope * qbase.astype(F32)), 0.0)).astype(BF16)
    z64 = jnp.zeros((DA_HEAD_DIM, tq), BF16)
    w1 = jnp.concatenate([qt[:DA_HEAD_DIM], z64, brow], axis=0)
    w2 = jnp.concatenate([z64, qt[DA_HEAD_DIM:], brow], axis=0)
    chains = [(w, hf) for w in (w1, w2) for hf in range(n_q)]
    ones_rows = jnp.where(lax.broadcasted_iota(jnp.int32, (V_ROWS - HEAD_W, tk), 0) == 0, 1.0, 0.0).astype(BF16)

    m_ref[...] = jnp.full(m_ref.shape, NEG, F32)
    a_ref[...] = jnp.zeros(a_ref.shape, F32)

    def tile(ki, modes):
        start = pl.multiple_of(ki * tk, tk)
        kt = jnp.concatenate([k_ref[pl.ds(start, tk), :], kpos_ref[pl.ds(start, tk), :]], axis=1)
        vt = jnp.concatenate([vt_ref[ki], ones_rows], axis=0)
        key = start + lax.broadcasted_iota(jnp.int32, (tk, Q_CHAIN), 0)
        lane = lax.broadcasted_iota(jnp.int32, (tk, Q_CHAIN), 1)
        live = [c for c, (_, hf) in enumerate(chains) if modes[hf] != "skip"]
        tile_max = {}
        for c in live:
            w, hf = chains[c]
            st = jnp.dot(kt, w[:, hf * Q_CHAIN:(hf + 1) * Q_CHAIN], preferred_element_type=F32)
            if modes[hf] == "front":
                st = jnp.where(key >= first_key, st, NEG)
            elif modes[hf] == "causal":
                st = jnp.where(key <= lane + (qbase + hf * Q_CHAIN), st, NEG)
            s_ref[c] = st
            tile_max[c] = jnp.max(st, axis=0, keepdims=True)
        for c in live:
            m_old = m_ref[c]
            m_new = jnp.maximum(m_old, tile_max[c])
            alpha = jnp.exp(m_old - m_new)
            p = jnp.exp(s_ref[c] - m_new).astype(BF16)
            a_ref[c] = alpha * a_ref[c] + jnp.dot(vt, p, preferred_element_type=F32)
            m_ref[c] = m_new

    half = n_q // 2
    tile(0, ["front"] * n_q)

    def body(ki, carry):
        tile(ki, ["none"] * n_q)
        return carry

    lax.fori_loop(1, 2 * j + 1, body, 0)
    tile(2 * j + 1, ["causal"] * half + ["none"] * half)
    tile(2 * j + 2, ["skip"] * half + ["causal"] * half)

    lt = lam_ref[...]
    lam = (jnp.exp(jnp.sum(lt[0:1] * lt[1:2], axis=-1, keepdims=True))
           - jnp.exp(jnp.sum(lt[2:3] * lt[3:4], axis=-1, keepdims=True)) + LAMBDA_INIT)
    for hf in range(n_q):
        a1 = a_ref[hf]
        a2 = a_ref[n_q + hf]
        ot = a1[:HEAD_W] / a1[HEAD_W:HEAD_W + 1] - lam * (a2[:HEAD_W] / a2[HEAD_W:HEAD_W + 1])
        o = ot.T
        o = o * lax.rsqrt(jnp.mean(o * o, axis=-1, keepdims=True) + NORM_EPS) * sw_ref[...]
        o_ref[hf * Q_CHAIN:(hf + 1) * Q_CHAIN, :] = (o * (1.0 - LAMBDA_INIT)).astype(BF16)


def _attention(qvt, proj, kpos, slopes, lam_tab, subln_w, seq, first_key):
    n_t = qvt.shape[0]
    lt = n_t * TILE
    tpq = ATT_TQ // TILE
    assert ATT_TQ == 2 * TILE
    kern = functools.partial(_attn_kernel, first_key=first_key)
    n_chain = 2 * ATT_TQ // Q_CHAIN
    scratch = [pltpu.VMEM((n_chain, 1, Q_CHAIN), F32), pltpu.VMEM((n_chain, V_ROWS, Q_CHAIN), F32),
               pltpu.VMEM((n_chain, TILE, Q_CHAIN), F32)]
    return pl.pallas_call(
        kern,
        grid=(N_HEADS, seq // ATT_TQ),
        in_specs=[
            pl.BlockSpec((1, HEAD_W, TILE), lambda h, j: (1 + tpq * j, h, 0)),
            pl.BlockSpec((1, HEAD_W, TILE), lambda h, j: (2 + tpq * j, h, 0)),
            pl.BlockSpec((lt, HEAD_W), lambda h, j: (0, CB_AK + h)),
            pl.BlockSpec((lt, LANES), lambda h, j: (0, 0)),
            pl.BlockSpec((n_t, HEAD_W, TILE), lambda h, j: (0, N_HEADS + h, 0)),
            pl.BlockSpec((8, LANES), lambda h, j: (0, 0)),
            pl.BlockSpec((8, LANES), lambda h, j: (0, 0)),
            pl.BlockSpec((1, HEAD_W), lambda h, j: (0, 0)),
        ],
        out_specs=pl.BlockSpec((ATT_TQ, HEAD_W), lambda h, j: (j, h)),
        out_shape=jax.ShapeDtypeStruct((seq, N_HEADS * HEAD_W), BF16),
        scratch_shapes=scratch,
        compiler_params=_cparams(("parallel", "arbitrary")),
        name="diff_attn",
    )(qvt, qvt, proj, kpos, qvt, slopes, lam_tab, subln_w)


def _dn_prep_kernel(x_ref, halo_ref, sm_ref, cw_ref, al_ref, dtb_ref,
                    q_ref, k_ref, v_ref, b_ref, g_ref, xe_scr, *, first_row, last_row):
    m = pl.program_id(0)
    tm = x_ref.shape[0]
    halo = halo_ref[...].astype(F32)
    xe_scr[0:8, :] = jnp.where(m > 0, halo, jnp.zeros_like(halo))
    xe_scr[8:, :] = x_ref[...].astype(F32)
    cw = cw_ref[...]
    acc = xe_scr[8:, :] * cw[DN_CONV - 1:DN_CONV, :]
    for j in range(DN_CONV - 1):
        acc = acc + xe_scr[pl.ds(8 - (DN_CONV - 1) + j, tm), :] * cw[j:j + 1, :]
    y = _silu(acc)

    rowid = m * tm + lax.broadcasted_iota(jnp.int32, (tm, LANES), 0)
    real = (rowid >= first_row) & (rowid < last_row)
    lane = lax.broadcasted_iota(jnp.int32, (tm, LANES), 1)
    sm = sm_ref[...]
    beta_all = _sigmoid(sm)
    a = sm + dtb_ref[...]
    softplus = jnp.maximum(a, 0.0) + jnp.log(1.0 + jnp.exp(-jnp.abs(a)))
    g_all = -jnp.exp(al_ref[...]) * softplus
    zeros = jnp.zeros((tm, LANES), F32)
    width = N_HEADS * HEAD_W
    for h in range(N_HEADS):
        sl = slice(h * HEAD_W, (h + 1) * HEAD_W)
        qh = y[:, sl]
        kh = y[:, width + h * HEAD_W: width + (h + 1) * HEAD_W]
        vh = y[:, 2 * width + h * HEAD_W: 2 * width + (h + 1) * HEAD_W]
        qh = qh * lax.rsqrt(jnp.sum(qh * qh, axis=-1, keepdims=True) + 1e-6) * (HEAD_W ** -0.5)
        kh = kh * lax.rsqrt(jnp.sum(kh * kh, axis=-1, keepdims=True) + 1e-6)
        q_ref[h] = qh.astype(BF16)
        k_ref[h] = kh.astype(BF16)
        v_ref[h] = vh.astype(BF16)
        bh = jnp.sum(jnp.where(lane == h, beta_all, zeros), axis=-1, keepdims=True)
        gh = jnp.sum(jnp.where(lane == N_HEADS + h, g_all, zeros), axis=-1, keepdims=True)
        b_ref[h] = jnp.where(real, jnp.broadcast_to(bh, (tm, LANES)), zeros)
        g_ref[h] = jnp.where(real, jnp.broadcast_to(gh, (tm, LANES)), zeros)


def _dn_prep(proj, small, conv_w, a_log_row, dtb_row, first_row, last_row):
    lt = proj.shape[0]
    c = 3 * N_HEADS * HEAD_W
    kern = functools.partial(_dn_prep_kernel, first_row=first_row, last_row=last_row)
    hshape = lambda dt: jax.ShapeDtypeStruct((N_HEADS, lt, HEAD_W), dt)
    hspec = pl.BlockSpec((N_HEADS, TILE, HEAD_W), lambda m: (0, m, 0))
    return pl.pallas_call(
        kern,
        grid=(lt // TILE,),
        in_specs=[
            pl.BlockSpec((TILE, c), lambda m: (m, CB_DQKV)),
            pl.BlockSpec((8, c), lambda m: (jnp.maximum(m * (TILE // 8) - 1, 0), CB_DQKV)),
            pl.BlockSpec((TILE, LANES), lambda m: (m, 0)),
            pl.BlockSpec((DN_CONV, c), lambda m: (0, 0)),
            pl.BlockSpec((1, LANES), lambda m: (0, 0)),
            pl.BlockSpec((1, LANES), lambda m: (0, 0)),
        ],
        out_specs=[hspec, hspec, hspec, hspec, hspec],
        out_shape=[hshape(BF16), hshape(BF16), hshape(BF16), hshape(F32), hshape(F32)],
        scratch_shapes=[pltpu.VMEM((TILE + 8, c), F32)],
        compiler_params=_cparams(("parallel",)),
        name="dn_prep",
    )(proj, proj, small, conv_w, a_log_row, dtb_row)


def _dn_chunk_kernel(q_ref, k_ref, v_ref, b_ref, g_ref, gate_ref, nw_ref, o_ref, s_scr):
    @pl.when(pl.program_id(1) == 0)
    def _():
        s_scr[...] = jnp.zeros(s_scr.shape, F32)

    r = DN_ROWS
    c = DN_CHUNK
    n_chunks = r // c
    hs = range(q_ref.shape[0])

    row = lax.broadcasted_iota(jnp.int32, (r, r), 0)
    col = lax.broadcasted_iota(jnp.int32, (r, r), 1)
    same64 = (row // c) == (col // c)
    same32 = (row // 32) == (col // 32)
    same16 = (row // 16) == (col // 16)
    incl = same64 & (row >= col)
    strict = same64 & (row > col)
    off16 = same32 & jnp.logical_not(same16)
    off32 = same64 & jnp.logical_not(same32)
    zero = jnp.zeros((r, r), F32)
    incl_b = jnp.where(incl, 1.0, 0.0).astype(BF16)
    eye = jnp.where(row == col, 1.0, 0.0)

    q = [q_ref[h].astype(F32) for h in hs]
    k = [k_ref[h].astype(F32) for h in hs]
    v = [v_ref[h].astype(F32) for h in hs]
    beta = [b_ref[h] for h in hs]

    def cumsum(g):
        g_hi = g.astype(BF16)
        r1 = g - g_hi.astype(F32)
        g_mid = r1.astype(BF16)
        g_lo = (r1 - g_mid.astype(F32)).astype(BF16)
        parts = jnp.dot(incl_b, jnp.concatenate([g_hi, g_mid, g_lo], axis=1), preferred_element_type=F32)
        return (parts[:, :LANES] + parts[:, LANES:2 * LANES]) + parts[:, 2 * LANES:]

    gc = [cumsum(g_ref[h]) for h in hs]
    gc_row = [gc[h].T[0:1, :] for h in hs]
    diff = [jnp.concatenate([gc[h], gc[h]], axis=1) - gc_row[h] for h in hs]
    decay = [jnp.where(incl, jnp.exp(jnp.where(incl, diff[h], zero)), zero) for h in hs]
    gtot = [jnp.concatenate([jnp.broadcast_to(gc[h][(i + 1) * c - 1:(i + 1) * c, :], (c, LANES))
                             for i in range(n_chunks)], axis=0) for h in hs]
    egc = [jnp.exp(gc[h]) for h in hs]
    kb = [k[h] * beta[h] for h in hs]
    a = [jnp.where(strict, _mm_nt(kb[h], k[h]) * decay[h], zero) for h in hs]

    ad = [jnp.where(same16, a[h], zero) for h in hs]
    p2 = [_mm(ad[h], ad[h]) for h in hs]
    p4 = [_mm(p2[h], p2[h]) for h in hs]
    p8 = [_mm(p4[h], p4[h]) for h in hs]
    t = [_mm(eye - ad[h], eye + p2[h]) for h in hs]
    t = [_mm(t[h], eye + p4[h]) for h in hs]
    t = [_mm(t[h], eye + p8[h]) for h in hs]
    x = [_mm(t[h], jnp.where(off16, a[h], zero)) for h in hs]
    t = [t[h] - _mm(x[h], t[h]) for h in hs]
    x = [_mm(t[h], jnp.where(off32, a[h], zero)) for h in hs]
    t = [t[h] - _mm(x[h], t[h]) for h in hs]

    uw = [_mm(t[h], jnp.concatenate([v[h] * beta[h], kb[h] * egc[h]], axis=1)) for h in hs]
    qk = [jnp.where(incl, _mm_nt(q[h], k[h]) * decay[h], zero) for h in hs]
    qd = [q[h] * egc[h] for h in hs]
    kd = [k[h] * jnp.exp(gtot[h] - gc[h]) for h in hs]
    gts = [jnp.exp(gtot[h]) for h in hs]

    s = [s_scr[h] for h in hs]
    vns = [[] for _ in hs]
    outs = [[] for _ in hs]
    for i in range(n_chunks):
        rs = slice(i * c, (i + 1) * c)
        vn = [uw[h][rs, :HEAD_W] - _mm(uw[h][rs, HEAD_W:], s[h]) for h in hs]
        for h in hs:
            vns[h].append(vn[h])
        pad = [jnp.zeros(((n_chunks - 1 - i) * c, HEAD_W), F32)] * (i < n_chunks - 1)
        vn_all = [jnp.concatenate(vns[h] + pad, axis=0) for h in hs]
        o_i = [_mm(qd[h][rs], s[h]) + _mm(qk[h][rs], vn_all[h]) for h in hs]
        for h in hs:
            outs[h].append(o_i[h])
        s = [s[h] * gts[h][i * c:i * c + 1, :] + _mm_tn(kd[h][rs], vn[h]) for h in hs]
    for h in hs:
        s_scr[h] = s[h]

    for h in hs:
        o = jnp.concatenate(outs[h], axis=0)
        o = o * lax.rsqrt(jnp.mean(o * o, axis=-1, keepdims=True) + NORM_EPS) * nw_ref[...]
        gate = gate_ref[:, h * HEAD_W:(h + 1) * HEAD_W].astype(F32)
        o_ref[:, h * HEAD_W:(h + 1) * HEAD_W] = (o * _silu(gate)).astype(BF16)


def _dn_chunks(qn, kn, vc, beta_b, g_b, proj, norm_w):
    lt = qn.shape[1]
    hg = DN_HEADS_PER_STEP
    first_blk = (TILE - DN_ROWS) // DN_ROWS
    hspec = pl.BlockSpec((hg, DN_ROWS, HEAD_W), lambda g, i: (g, i + first_blk, 0))
    wide = hg * HEAD_W
    return pl.pallas_call(
        _dn_chunk_kernel,
        grid=(N_HEADS // hg, lt // DN_ROWS - first_blk),
        in_specs=[
            hspec, hspec, hspec, hspec, hspec,
            pl.BlockSpec((DN_ROWS, wide), lambda g, i: (i + first_blk, CB_DG * HEAD_W // wide + g)),
            pl.BlockSpec((1, HEAD_W), lambda g, i: (0, 0)),
        ],
        out_specs=pl.BlockSpec((DN_ROWS, wide), lambda g, i: (i + first_blk, g)),
        out_shape=jax.ShapeDtypeStruct((lt, N_HEADS * HEAD_W), BF16),
        scratch_shapes=[pltpu.VMEM((hg, HEAD_W, HEAD_W), F32)],
        compiler_params=_cparams(("parallel", "arbitrary")),
        name="dn_chunks",
    )(qn, kn, vc, beta_b, g_b, proj, norm_w)


def _mix_kernel(ya_ref, yd_ref, ga_ref, gd_ref, h_ref, wa_ref, wd_ref, wo_ref, nw_ref, wr_ref, br_ref,
                h1_ref, u2_ref, lg_ref):
    pa = jnp.dot(ya_ref[...], wa_ref[...], preferred_element_type=F32)
    pd = jnp.dot(yd_ref[...], wd_ref[...], preferred_element_type=F32)
    mixed = _sigmoid(ga_ref[...].astype(F32)) * pa + _sigmoid(gd_ref[...].astype(F32)) * pd
    h1 = h_ref[...] + jnp.dot(mixed.astype(BF16), wo_ref[...], preferred_element_type=F32)
    h1_ref[...] = h1
    u2 = h1 * lax.rsqrt(jnp.mean(h1 * h1, axis=-1, keepdims=True) + NORM_EPS) * nw_ref[...]
    u2_ref[...] = u2.astype(BF16)
    lg_ref[...] = _mm_f32(u2, wr_ref[...]) + br_ref[...]


def _mix(y_a, y_d, proj, hp, w_a, w_d, w_o, norm_w, w_r, b_r):
    seq = y_a.shape[0]
    skip = (hp.shape[0] - seq) // TILE
    row = lambda: pl.BlockSpec((TILE, D_MODEL), lambda m: (m, 0))
    prow = lambda cb: pl.BlockSpec((TILE, D_MODEL), lambda m: (m + skip, cb))
    full = lambda: pl.BlockSpec((D_MODEL, D_MODEL), lambda m: (0, 0))
    return pl.pallas_call(
        _mix_kernel,
        grid=(seq // TILE,),
        in_specs=[
            row(), prow(0), prow(CB_GA // 8), prow(CB_GD // 8), prow(0), full(), full(), full(),
            pl.BlockSpec((1, D_MODEL), lambda m: (0, 0)),
            pl.BlockSpec((D_MODEL, LANES), lambda m: (0, 0)),
            pl.BlockSpec((1, LANES), lambda m: (0, 0)),
        ],
        out_specs=[row(), row(), pl.BlockSpec((TILE, LANES), lambda m: (m, 0))],
        out_shape=[
            jax.ShapeDtypeStruct((seq, D_MODEL), F32),
            jax.ShapeDtypeStruct((seq, D_MODEL), BF16),
            jax.ShapeDtypeStruct((seq, LANES), F32),
        ],
        compiler_params=_cparams(("parallel",)),
        name="branch_mix",
    )(y_a, y_d, proj, proj, hp, w_a, w_d, w_o, norm_w, w_r, b_r)


def _route(logits):
    shape = logits.shape
    lane = lax.broadcasted_iota(jnp.int32, shape, 1)
    big = jnp.full(shape, 4 * LANES, jnp.int32)
    neg = jnp.full(shape, NEG, F32)
    zero = jnp.zeros(shape, F32)
    is_g = lane < N_GROUPS
    gl = jnp.where(is_g, logits, neg)
    gmax = jnp.max(gl, axis=-1, keepdims=True)
    gsum = jnp.sum(jnp.where(is_g, jnp.exp(gl - gmax), zero), axis=-1, keepdims=True)
    g_w = 1.0 / gsum
    g_i = jnp.min(jnp.where(is_g & (gl == gmax), lane, big), axis=-1, keepdims=True)
    e_lane = lane - N_GROUPS
    in_grp = (e_lane >= 0) & (e_lane < N_EXPERTS) & ((e_lane // EXPERTS_PER_GROUP) == g_i)
    el = jnp.where(in_grp, logits, neg)
    emax = jnp.max(el, axis=-1, keepdims=True)
    eexp = jnp.where(in_grp, jnp.exp(el - emax), zero)
    ep = eexp / jnp.sum(eexp, axis=-1, keepdims=True)
    minus = jnp.full(shape, -1.0, F32)
    ep1 = jnp.where(in_grp, ep, minus)
    p1 = jnp.max(ep1, axis=-1, keepdims=True)
    i1 = jnp.min(jnp.where(ep1 == p1, lane, big), axis=-1, keepdims=True)
    ep2 = jnp.where(lane == i1, minus, ep1)
    p2 = jnp.max(ep2, axis=-1, keepdims=True)
    i2 = jnp.min(jnp.where(ep2 == p2, lane, big), axis=-1, keepdims=True)
    den = p1 + p2
    return g_w * (jnp.where(lane == i1, p1 / den, zero) + jnp.where(lane == i2, p2 / den, zero))


def _moe_kernel(u_ref, lg_ref, h1_ref, wg_ref, wu_ref, wd_ref, fw_ref, o_ref, c_scr, acc_scr):
    e = pl.program_id(1)

    @pl.when(e == 0)
    def _():
        c_scr[...] = _route(lg_ref[...])
        acc_scr[...] = jnp.zeros(acc_scr.shape, F32)

    x = u_ref[...]
    a = jnp.dot(x, wg_ref[0], preferred_element_type=F32)
    b = jnp.dot(x, wu_ref[0], preferred_element_type=F32)
    cmb = c_scr[...]
    lane = lax.broadcasted_iota(jnp.int32, cmb.shape, 1)
    ce = jnp.sum(jnp.where(lane == e + N_GROUPS, cmb, jnp.zeros_like(cmb)), axis=-1, keepdims=True)
    hh = _silu(a) * b
    y = jnp.dot(hh.astype(BF16), wd_ref[0], preferred_element_type=F32)
    acc_scr[...] += ce * y

    @pl.when(e == N_EXPERTS - 1)
    def _():
        hf = h1_ref[...] + acc_scr[...]
        o_ref[...] = hf * lax.rsqrt(jnp.mean(hf * hf, axis=-1, keepdims=True) + NORM_EPS) * fw_ref[...]


def _moe(u2, logits, h1, w_gate, w_up, w_down, final_w):
    seq = u2.shape[0]
    row = lambda w: pl.BlockSpec((TILE, w), lambda m, e: (m, 0))
    return pl.pallas_call(
        _moe_kernel,
        grid=(seq // TILE, N_EXPERTS),
        in_specs=[
            row(D_MODEL), row(LANES), row(D_MODEL),
            pl.BlockSpec((1, D_MODEL, D_EXPERT), lambda m, e: (e, 0, 0)),
            pl.BlockSpec((1, D_MODEL, D_EXPERT), lambda m, e: (e, 0, 0)),
            pl.BlockSpec((1, D_EXPERT, D_MODEL), lambda m, e: (e, 0, 0)),
            pl.BlockSpec((1, D_MODEL), lambda m, e: (0, 0)),
        ],
        out_specs=row(D_MODEL),
        out_shape=jax.ShapeDtypeStruct((seq, D_MODEL), F32),
        scratch_shapes=[pltpu.VMEM((TILE, LANES), F32), pltpu.VMEM((TILE, D_MODEL), F32)],
        compiler_params=_cparams(("parallel", "arbitrary")),
        name="moe_final",
    )(u2, logits, h1, w_gate, w_up, w_down, final_w)


def kernel(x, meta_tokens, norm1_w, w_in, lambda_q1, lambda_k1, lambda_q2, lambda_k2, da_subln_w, dn_conv_w,
           dn_A_log, dn_dt_bias, dn_norm_w, w_branch_attn, w_branch_dn, w_out, norm2_w, router_group_w,
           router_group_b, router_expert_w, router_expert_b, expert_w_gate, expert_w_up, expert_w_down,
           final_norm_w):
    assert x.shape[0] == 1 and x.shape[2] == D_MODEL and x.shape[1] % ATT_TQ == 0
    assert w_in.shape[0] == 1, "single layer: the block output at the meta rows is unused"
    seq = x.shape[1]
    lt = TILE + seq
    assert lt <= 256 * LANES, "key positions must split into two exact bf16 terms"
    first = TILE - N_META

    hp = jnp.concatenate([jnp.zeros((first, D_MODEL), F32), meta_tokens.astype(F32), x[0]], axis=0)

    w = w_in[0]
    d = D_MODEL
    w_big = jnp.concatenate([w[:, 3 * d:6 * d], w[:, d:2 * d], w[:, 6 * d:7 * d], w[:, 7 * d + 2 * N_HEADS:]],
                            axis=1).astype(BF16)
    w_qv_t = jnp.concatenate([w[:, :d] * DA_HEAD_DIM ** -0.5, w[:, 2 * d:3 * d]], axis=1).T.astype(BF16)
    w_small = jnp.pad(w[:, 7 * d:7 * d + 2 * N_HEADS], ((0, 0), (0, LANES - 2 * N_HEADS)))
    proj, qvt, small = _inproj(hp, norm1_w, w_big, w_qv_t, w_small)

    pos = jnp.arange(lt, dtype=jnp.int32)
    kpos = jnp.stack([(pos >> 7) << 7, pos & 127, jnp.ones_like(pos)], axis=1).astype(BF16)
    kpos = jnp.pad(kpos, ((0, 0), (0, LANES - 3)))
    slopes = jnp.broadcast_to(jnp.asarray(ALIBI_SLOPES, F32)[:, None], (N_HEADS, LANES))
    lam_tab = jnp.pad(jnp.concatenate([lambda_q1, lambda_k1, lambda_q2, lambda_k2], axis=0),
                      ((0, 4), (0, LANES - DA_HEAD_DIM)))
    y_a = _attention(qvt, proj, kpos, slopes, lam_tab, da_subln_w, seq, first)

    pad16 = lambda t, off: jnp.pad(t, ((0, 0), (off, LANES - N_HEADS - off)))
    qn, kn, vc, beta_b, g_b = _dn_prep(proj, small, dn_conv_w[0], pad16(dn_A_log, N_HEADS),
                                       pad16(dn_dt_bias, N_HEADS), first, lt)
    y_d = _dn_chunks(qn, kn, vc, beta_b, g_b, proj, dn_norm_w)

    w_r = jnp.pad(jnp.concatenate([router_group_w[0], router_expert_w[0]], axis=1),
                  ((0, 0), (0, LANES - N_GROUPS - N_EXPERTS)))
    b_r = jnp.pad(jnp.concatenate([router_group_b, router_expert_b], axis=1),
                  ((0, 0), (0, LANES - N_GROUPS - N_EXPERTS)))
    h1, u2, logits = _mix(y_a, y_d, proj, hp, w_branch_attn[0].astype(BF16), w_branch_dn[0].astype(BF16),
                          w_out[0].astype(BF16), norm2_w, w_r, b_r)

    out = _moe(u2, logits, h1, expert_w_gate[0].astype(BF16), expert_w_up[0].astype(BF16),
               expert_w_down[0].astype(BF16), final_norm_w[None, :])
    return out[None]
```

```python
import functools
import math

import jax
import jax.numpy as jnp
from jax import lax
from jax.experimental import pallas as pl
from jax.experimental.pallas import tpu as pltpu

F32 = jnp.float32
BF16 = jnp.bfloat16
HIGHEST = lax.Precision.HIGHEST

D_MODEL = 1024
N_META = 16
NORM_EPS = 1e-6
N_HEADS = 8
HEAD_W = 128
DA_HEAD_DIM = 64
DN_CONV = 4
DN_CHUNK = 64
N_GROUPS = 4
EXPERTS_PER_GROUP = 8
N_EXPERTS = 32
D_EXPERT = 256
LAMBDA_INIT = 0.8 - 0.6 * math.exp(-0.3 * 0)
ALIBI_SLOPES = tuple(2.0 ** (-8.0 * (h + 1) / N_HEADS) for h in range(N_HEADS))

TILE = 512
ATT_TQ = 1024
Q_CHAIN = 256
V_ROWS = 144
DN_ROWS = 256
DN_HEADS_PER_STEP = 4
LANES = 128
NEG = -1e30
VMEM_LIMIT = 56 * 1024 * 1024
MOE_TM = 1024
MOE_CHUNK = 128
MOE_CAP = MOE_TM + N_GROUPS * MOE_CHUNK
MOE_VMEM_LIMIT = 60 * 1024 * 1024

CB_DQKV, CB_AK, CB_DG, CB_GA, CB_GD = 0, 24, 32, 40, 48
PROJ_COLS = 56 * 128
PROJ_TN = PROJ_COLS // 2


def _cparams(sem):
    return pltpu.CompilerParams(dimension_semantics=sem, vmem_limit_bytes=VMEM_LIMIT)


def _mm(a, b):
    return jnp.dot(a.astype(BF16), b.astype(BF16), preferred_element_type=F32)


def _mm_nt(a, b):
    return lax.dot_general(a.astype(BF16), b.astype(BF16), (((1,), (1,)), ((), ())),
                           preferred_element_type=F32)


def _mm_tn(a, b):
    return lax.dot_general(a.astype(BF16), b.astype(BF16), (((0,), (0,)), ((), ())),
                           preferred_element_type=F32)


def _mm_f32(a, b):
    return jnp.dot(a, b, precision=HIGHEST, preferred_element_type=F32)


def _silu(x):
    return x * (1.0 / (1.0 + jnp.exp(-x)))


def _sigmoid(x):
    return 1.0 / (1.0 + jnp.exp(-x))


def _inproj_kernel(x_ref, nw_ref, w_ref, wt_ref, ws_ref, o_ref, ot_ref, os_ref, u_scr):
    @pl.when(pl.program_id(1) == 0)
    def _():
        x = x_ref[...]
        u = x * lax.rsqrt(jnp.mean(x * x, axis=-1, keepdims=True) + NORM_EPS) * nw_ref[...]
        ub = u.astype(BF16)
        u_scr[...] = ub
        os_ref[...] = _mm_f32(u, ws_ref[...])
        ot_ref[0] = lax.dot_general(wt_ref[...], ub, (((1,), (1,)), ((), ())),
                                    preferred_element_type=F32).astype(BF16)

    o_ref[...] = jnp.dot(u_scr[...], w_ref[...], preferred_element_type=F32).astype(BF16)


def _inproj(hp, norm_w, w_big, w_qv_t, w_small):
    lt = hp.shape[0]
    n_t = lt // TILE
    return pl.pallas_call(
        _inproj_kernel,
        grid=(n_t, PROJ_COLS // PROJ_TN),
        in_specs=[
            pl.BlockSpec((TILE, D_MODEL), lambda m, n: (m, 0)),
            pl.BlockSpec((1, D_MODEL), lambda m, n: (0, 0)),
            pl.BlockSpec((D_MODEL, PROJ_TN), lambda m, n: (0, n)),
            pl.BlockSpec((2 * D_MODEL, D_MODEL), lambda m, n: (0, 0)),
            pl.BlockSpec((D_MODEL, LANES), lambda m, n: (0, 0)),
        ],
        out_specs=[
            pl.BlockSpec((TILE, PROJ_TN), lambda m, n: (m, n)),
            pl.BlockSpec((1, 2 * D_MODEL, TILE), lambda m, n: (m, 0, 0)),
            pl.BlockSpec((TILE, LANES), lambda m, n: (m, 0)),
        ],
        out_shape=[
            jax.ShapeDtypeStruct((lt, PROJ_COLS), BF16),
            jax.ShapeDtypeStruct((n_t, 2 * D_MODEL, TILE), BF16),
            jax.ShapeDtypeStruct((lt, LANES), F32),
        ],
        scratch_shapes=[pltpu.VMEM((TILE, D_MODEL), BF16)],
        compiler_params=_cparams(("parallel", "arbitrary")),
        name="inproj",
    )(hp, norm_w, w_big, w_qv_t, w_small)


def _attn_kernel(q0_ref, q1_ref, k_ref, kpos_ref, vt_ref, slope_ref, lam_ref, sw_ref, o_ref,
                 m_ref, a_ref, s_ref, *, first_key):
    h = pl.program_id(0)
    j = pl.program_id(1)
    tk = TILE
    tq = ATT_TQ
    n_q = tq // Q_CHAIN

    qt = jnp.concatenate([q0_ref[0], q1_ref[0]], axis=1)
    slope = slope_ref[pl.ds(h, 1), :][:, :1]
    qbase = TILE + j * tq
    rowi = lax.broadcasted_iota(jnp.int32, (HEAD_W, tq), 0)
    brow = jnp.where(rowi < 2, slope, jnp.where(rowi == 2, -(slope * qbase.astype(F32)), 0.0)).astype(BF16)
    z64 = jnp.zeros((DA_HEAD_DIM, tq), BF16)
    w1 = jnp.concatenate([qt[:DA_HEAD_DIM], z64, brow], axis=0)
    w2 = jnp.concatenate([z64, qt[DA_HEAD_DIM:], brow], axis=0)
    chains = [(w, hf) for w in (w1, w2) for hf in range(n_q)]
    ones_rows = jnp.where(lax.broadcasted_iota(jnp.int32, (V_ROWS - HEAD_W, tk), 0) == 0, 1.0, 0.0).astype(BF16)

    m_ref[...] = jnp.full(m_ref.shape, NEG, F32)
    a_ref[...] = jnp.zeros(a_ref.shape, F32)

    def tile(ki, modes, lo=0, rows=tk):
        start = pl.multiple_of(ki * tk, tk) + lo
        kt = jnp.concatenate([k_ref[pl.ds(start, rows), :], kpos_ref[pl.ds(start, rows), :]], axis=1)
        vt = jnp.concatenate([vt_ref[ki][:, lo:lo + rows], ones_rows[:, :rows]], axis=0)
        key = start + lax.broadcasted_iota(jnp.int32, (rows, Q_CHAIN), 0)
        lane = lax.broadcasted_iota(jnp.int32, (rows, Q_CHAIN), 1)
        live = [c for c, (_, hf) in enumerate(chains) if modes[hf] != "skip"]
        tile_max = {}
        for c in live:
            w, hf = chains[c]
            st = jnp.dot(kt, w[:, hf * Q_CHAIN:(hf + 1) * Q_CHAIN], preferred_element_type=F32)
            if modes[hf] == "front":
                st = jnp.where(key >= first_key, st, NEG)
            elif modes[hf] == "causal":
                st = jnp.where(key <= lane + (qbase + hf * Q_CHAIN), st, NEG)
            s_ref[c, :rows] = st
            tile_max[c] = jnp.max(st, axis=0, keepdims=True)
        for c in live:
            m_old = m_ref[c]
            m_new = jnp.maximum(m_old, tile_max[c])
            alpha = jnp.exp(m_old - m_new)
            p = jnp.exp(s_ref[c, :rows] - m_new).astype(BF16)
            a_ref[c] = alpha * a_ref[c] + jnp.dot(vt, p, preferred_element_type=F32)
            m_ref[c] = m_new

    half = n_q // 2
    tile(0, ["front"] * n_q, lo=tk - LANES, rows=LANES)

    def body(ki, carry):
        tile(ki, ["none"] * n_q)
        return carry

    lax.fori_loop(1, 2 * j + 1, body, 0)
    tile(2 * j + 1, ["causal"] * half + ["none"] * half)
    tile(2 * j + 2, ["skip"] * half + ["causal"] * half)

    lt = lam_ref[...]
    lam = (jnp.exp(jnp.sum(lt[0:1] * lt[1:2], axis=-1, keepdims=True))
           - jnp.exp(jnp.sum(lt[2:3] * lt[3:4], axis=-1, keepdims=True)) + LAMBDA_INIT)
    for hf in range(n_q):
        a1 = a_ref[hf]
        a2 = a_ref[n_q + hf]
        ot = a1[:HEAD_W] / a1[HEAD_W:HEAD_W + 1] - lam * (a2[:HEAD_W] / a2[HEAD_W:HEAD_W + 1])
        o = ot.T
        o = o * lax.rsqrt(jnp.mean(o * o, axis=-1, keepdims=True) + NORM_EPS) * sw_ref[...]
        o_ref[hf * Q_CHAIN:(hf + 1) * Q_CHAIN, :] = (o * (1.0 - LAMBDA_INIT)).astype(BF16)


def _attention(qvt, proj, kpos, slopes, lam_tab, subln_w, seq, first_key):
    n_t = qvt.shape[0]
    lt = n_t * TILE
    tpq = ATT_TQ // TILE
    assert ATT_TQ == 2 * TILE and first_key >= TILE - LANES
    kern = functools.partial(_attn_kernel, first_key=first_key)
    n_chain = 2 * ATT_TQ // Q_CHAIN
    scratch = [pltpu.VMEM((n_chain, 1, Q_CHAIN), F32), pltpu.VMEM((n_chain, V_ROWS, Q_CHAIN), F32),
               pltpu.VMEM((n_chain, TILE, Q_CHAIN), F32)]
    return pl.pallas_call(
        kern,
        grid=(N_HEADS, seq // ATT_TQ),
        in_specs=[
            pl.BlockSpec((1, HEAD_W, TILE), lambda h, j: (1 + tpq * j, h, 0)),
            pl.BlockSpec((1, HEAD_W, TILE), lambda h, j: (2 + tpq * j, h, 0)),
            pl.BlockSpec((lt, HEAD_W), lambda h, j: (0, CB_AK + h)),
            pl.BlockSpec((lt, LANES), lambda h, j: (0, 0)),
            pl.BlockSpec((n_t, HEAD_W, TILE), lambda h, j: (0, N_HEADS + h, 0)),
            pl.BlockSpec((8, LANES), lambda h, j: (0, 0)),
            pl.BlockSpec((8, LANES), lambda h, j: (0, 0)),
            pl.BlockSpec((1, HEAD_W), lambda h, j: (0, 0)),
        ],
        out_specs=pl.BlockSpec((ATT_TQ, HEAD_W), lambda h, j: (j, h)),
        out_shape=jax.ShapeDtypeStruct((seq, N_HEADS * HEAD_W), BF16),
        scratch_shapes=scratch,
        compiler_params=_cparams(("parallel", "arbitrary")),
        name="diff_attn",
    )(qvt, qvt, proj, kpos, qvt, slopes, lam_tab, subln_w)


def _dn_prep_kernel(x_ref, halo_ref, sm_ref, cw_ref, al_ref, dtb_ref,
                    q_ref, k_ref, v_ref, b_ref, g_ref, xe_scr, *, first_row, last_row):
    m = pl.program_id(0)
    tm = x_ref.shape[0]
    halo = halo_ref[...].astype(F32)
    xe_scr[0:8, :] = jnp.where(m > 0, halo, jnp.zeros_like(halo))
    xe_scr[8:, :] = x_ref[...].astype(F32)
    cw = cw_ref[...]
    acc = xe_scr[8:, :] * cw[DN_CONV - 1:DN_CONV, :]
    for j in range(DN_CONV - 1):
        acc = acc + xe_scr[pl.ds(8 - (DN_CONV - 1) + j, tm), :] * cw[j:j + 1, :]
    y = _silu(acc)

    rowid = m * tm + lax.broadcasted_iota(jnp.int32, (tm, LANES), 0)
    real = (rowid >= first_row) & (rowid < last_row)
    lane = lax.broadcasted_iota(jnp.int32, (tm, LANES), 1)
    sm = sm_ref[...]
    beta_all = _sigmoid(sm)
    a = sm + dtb_ref[...]
    softplus = jnp.maximum(a, 0.0) + jnp.log(1.0 + jnp.exp(-jnp.abs(a)))
    g_all = -jnp.exp(al_ref[...]) * softplus
    zeros = jnp.zeros((tm, LANES), F32)
    width = N_HEADS * HEAD_W
    for h in range(N_HEADS):
        sl = slice(h * HEAD_W, (h + 1) * HEAD_W)
        qh = y[:, sl]
        kh = y[:, width + h * HEAD_W: width + (h + 1) * HEAD_W]
        vh = y[:, 2 * width + h * HEAD_W: 2 * width + (h + 1) * HEAD_W]
        qh = qh * lax.rsqrt(jnp.sum(qh * qh, axis=-1, keepdims=True) + 1e-6) * (HEAD_W ** -0.5)
        kh = kh * lax.rsqrt(jnp.sum(kh * kh, axis=-1, keepdims=True) + 1e-6)
        q_ref[h] = qh.astype(BF16)
        k_ref[h] = kh.astype(BF16)
        v_ref[h] = vh.astype(BF16)
        bh = jnp.sum(jnp.where(lane == h, beta_all, zeros), axis=-1, keepdims=True)
        gh = jnp.sum(jnp.where(lane == N_HEADS + h, g_all, zeros), axis=-1, keepdims=True)
        b_ref[h] = jnp.where(real, jnp.broadcast_to(bh, (tm, LANES)), zeros)
        g_ref[h] = jnp.where(real, jnp.broadcast_to(gh, (tm, LANES)), zeros)


def _dn_prep(proj, small, conv_w, a_log_row, dtb_row, first_row, last_row):
    lt = proj.shape[0]
    c = 3 * N_HEADS * HEAD_W
    kern = functools.partial(_dn_prep_kernel, first_row=first_row, last_row=last_row)
    hshape = lambda dt: jax.ShapeDtypeStruct((N_HEADS, lt, HEAD_W), dt)
    hspec = pl.BlockSpec((N_HEADS, TILE, HEAD_W), lambda m: (0, m, 0))
    return pl.pallas_call(
        kern,
        grid=(lt // TILE,),
        in_specs=[
            pl.BlockSpec((TILE, c), lambda m: (m, CB_DQKV)),
            pl.BlockSpec((8, c), lambda m: (jnp.maximum(m * (TILE // 8) - 1, 0), CB_DQKV)),
            pl.BlockSpec((TILE, LANES), lambda m: (m, 0)),
            pl.BlockSpec((DN_CONV, c), lambda m: (0, 0)),
            pl.BlockSpec((1, LANES), lambda m: (0, 0)),
            pl.BlockSpec((1, LANES), lambda m: (0, 0)),
        ],
        out_specs=[hspec, hspec, hspec, hspec, hspec],
        out_shape=[hshape(BF16), hshape(BF16), hshape(BF16), hshape(F32), hshape(F32)],
        scratch_shapes=[pltpu.VMEM((TILE + 8, c), F32)],
        compiler_params=_cparams(("parallel",)),
        name="dn_prep",
    )(proj, proj, small, conv_w, a_log_row, dtb_row)


def _dn_chunk_kernel(q_ref, k_ref, v_ref, b_ref, g_ref, gate_ref, nw_ref, o_ref, s_scr):
    @pl.when(pl.program_id(1) == 0)
    def _():
        s_scr[...] = jnp.zeros(s_scr.shape, F32)

    r = DN_ROWS
    c = DN_CHUNK
    n_chunks = r // c
    hs = range(q_ref.shape[0])

    row = lax.broadcasted_iota(jnp.int32, (r, r), 0)
    col = lax.broadcasted_iota(jnp.int32, (r, r), 1)
    same64 = (row // c) == (col // c)
    same32 = (row // 32) == (col // 32)
    same16 = (row // 16) == (col // 16)
    incl = same64 & (row >= col)
    strict = same64 & (row > col)
    off16 = same32 & jnp.logical_not(same16)
    off32 = same64 & jnp.logical_not(same32)
    zero = jnp.zeros((r, r), F32)
    incl_b = jnp.where(incl, 1.0, 0.0).astype(BF16)
    eye = jnp.where(row == col, 1.0, 0.0)

    q = [q_ref[h].astype(F32) for h in hs]
    k = [k_ref[h].astype(F32) for h in hs]
    v = [v_ref[h].astype(F32) for h in hs]
    beta = [b_ref[h] for h in hs]

    def cumsum(g):
        g_hi = g.astype(BF16)
        r1 = g - g_hi.astype(F32)
        g_mid = r1.astype(BF16)
        g_lo = (r1 - g_mid.astype(F32)).astype(BF16)
        parts = jnp.dot(incl_b, jnp.concatenate([g_hi, g_mid, g_lo], axis=1), preferred_element_type=F32)
        return (parts[:, :LANES] + parts[:, LANES:2 * LANES]) + parts[:, 2 * LANES:]

    gc = [cumsum(g_ref[h]) for h in hs]
    gc_row = [gc[h].T[0:1, :] for h in hs]
    diff = [jnp.concatenate([gc[h], gc[h]], axis=1) - gc_row[h] for h in hs]
    decay = [jnp.where(incl, jnp.exp(jnp.where(incl, diff[h], zero)), zero) for h in hs]
    gtot = [jnp.concatenate([jnp.broadcast_to(gc[h][(i + 1) * c - 1:(i + 1) * c, :], (c, LANES))
                             for i in range(n_chunks)], axis=0) for h in hs]
    egc = [jnp.exp(gc[h]) for h in hs]
    kb = [k[h] * beta[h] for h in hs]
    a = [jnp.where(strict, _mm_nt(kb[h], k[h]) * decay[h], zero) for h in hs]

    ad = [jnp.where(same16, a[h], zero) for h in hs]
    p2 = [_mm(ad[h], ad[h]) for h in hs]
    p4 = [_mm(p2[h], p2[h]) for h in hs]
    p8 = [_mm(p4[h], p4[h]) for h in hs]
    t = [_mm(eye - ad[h], eye + p2[h]) for h in hs]
    t = [_mm(t[h], eye + p4[h]) for h in hs]
    t = [_mm(t[h], eye + p8[h]) for h in hs]
    x = [_mm(t[h], jnp.where(off16, a[h], zero)) for h in hs]
    t = [t[h] - _mm(x[h], t[h]) for h in hs]
    x = [_mm(t[h], jnp.where(off32, a[h], zero)) for h in hs]
    t = [t[h] - _mm(x[h], t[h]) for h in hs]

    uw = [_mm(t[h], jnp.concatenate([v[h] * beta[h], kb[h] * egc[h]], axis=1)) for h in hs]
    qk = [jnp.where(incl, _mm_nt(q[h], k[h]) * decay[h], zero) for h in hs]
    qd = [q[h] * egc[h] for h in hs]
    kd = [k[h] * jnp.exp(gtot[h] - gc[h]) for h in hs]
    gts = [jnp.exp(gtot[h]) for h in hs]

    s = [s_scr[h] for h in hs]
    vns = [[] for _ in hs]
    outs = [[] for _ in hs]
    for i in range(n_chunks):
        rs = slice(i * c, (i + 1) * c)
        vn = [uw[h][rs, :HEAD_W] - _mm(uw[h][rs, HEAD_W:], s[h]) for h in hs]
        for h in hs:
            vns[h].append(vn[h])
        pad = [jnp.zeros(((n_chunks - 1 - i) * c, HEAD_W), F32)] * (i < n_chunks - 1)
        vn_all = [jnp.concatenate(vns[h] + pad, axis=0) for h in hs]
        o_i = [_mm(qd[h][rs], s[h]) + _mm(qk[h][rs], vn_all[h]) for h in hs]
        for h in hs:
            outs[h].append(o_i[h])
        s = [s[h] * gts[h][i * c:i * c + 1, :] + _mm_tn(kd[h][rs], vn[h]) for h in hs]
    for h in hs:
        s_scr[h] = s[h]

    for h in hs:
        o = jnp.concatenate(outs[h], axis=0)
        o = o * lax.rsqrt(jnp.mean(o * o, axis=-1, keepdims=True) + NORM_EPS) * nw_ref[...]
        gate = gate_ref[:, h * HEAD_W:(h + 1) * HEAD_W].astype(F32)
        o_ref[:, h * HEAD_W:(h + 1) * HEAD_W] = (o * _silu(gate)).astype(BF16)


def _dn_chunks(qn, kn, vc, beta_b, g_b, proj, norm_w):
    lt = qn.shape[1]
    hg = DN_HEADS_PER_STEP
    first_blk = (TILE - DN_ROWS) // DN_ROWS
    hspec = pl.BlockSpec((hg, DN_ROWS, HEAD_W), lambda g, i: (g, i + first_blk, 0))
    wide = hg * HEAD_W
    return pl.pallas_call(
        _dn_chunk_kernel,
        grid=(N_HEADS // hg, lt // DN_ROWS - first_blk),
        in_specs=[
            hspec, hspec, hspec, hspec, hspec,
            pl.BlockSpec((DN_ROWS, wide), lambda g, i: (i + first_blk, CB_DG * HEAD_W // wide + g)),
            pl.BlockSpec((1, HEAD_W), lambda g, i: (0, 0)),
        ],
        out_specs=pl.BlockSpec((DN_ROWS, wide), lambda g, i: (i + first_blk, g)),
        out_shape=jax.ShapeDtypeStruct((lt, N_HEADS * HEAD_W), BF16),
        scratch_shapes=[pltpu.VMEM((hg, HEAD_W, HEAD_W), F32)],
        compiler_params=_cparams(("parallel", "arbitrary")),
        name="dn_chunks",
    )(qn, kn, vc, beta_b, g_b, proj, norm_w)


def _mix_kernel(ya_ref, yd_ref, ga_ref, gd_ref, h_ref, wa_ref, wd_ref, wo_ref, nw_ref, wr_ref, br_ref,
                h1_ref, u2_ref, lg_ref):
    pa = jnp.dot(ya_ref[...], wa_ref[...], preferred_element_type=F32)
    pd = jnp.dot(yd_ref[...], wd_ref[...], preferred_element_type=F32)
    mixed = _sigmoid(ga_ref[...].astype(F32)) * pa + _sigmoid(gd_ref[...].astype(F32)) * pd
    h1 = h_ref[...] + jnp.dot(mixed.astype(BF16), wo_ref[...], preferred_element_type=F32)
    h1_ref[...] = h1
    u2 = h1 * lax.rsqrt(jnp.mean(h1 * h1, axis=-1, keepdims=True) + NORM_EPS) * nw_ref[...]
    u2_ref[...] = u2.astype(BF16)
    lg_ref[...] = _mm_f32(u2, wr_ref[...]) + br_ref[...]


def _mix(y_a, y_d, proj, hp, w_a, w_d, w_o, norm_w, w_r, b_r):
    seq = y_a.shape[0]
    skip = (hp.shape[0] - seq) // TILE
    row = lambda: pl.BlockSpec((TILE, D_MODEL), lambda m: (m, 0))
    prow = lambda cb: pl.BlockSpec((TILE, D_MODEL), lambda m: (m + skip, cb))
    full = lambda: pl.BlockSpec((D_MODEL, D_MODEL), lambda m: (0, 0))
    return pl.pallas_call(
        _mix_kernel,
        grid=(seq // TILE,),
        in_specs=[
            row(), prow(0), prow(CB_GA // 8), prow(CB_GD // 8), prow(0), full(), full(), full(),
            pl.BlockSpec((1, D_MODEL), lambda m: (0, 0)),
            pl.BlockSpec((D_MODEL, LANES), lambda m: (0, 0)),
            pl.BlockSpec((1, LANES), lambda m: (0, 0)),
        ],
        out_specs=[row(), row(), pl.BlockSpec((TILE, LANES), lambda m: (m, 0))],
        out_shape=[
            jax.ShapeDtypeStruct((seq, D_MODEL), F32),
            jax.ShapeDtypeStruct((seq, D_MODEL), BF16),
            jax.ShapeDtypeStruct((seq, LANES), F32),
        ],
        compiler_params=_cparams(("parallel",)),
        name="branch_mix",
    )(y_a, y_d, proj, proj, hp, w_a, w_d, w_o, norm_w, w_r, b_r)


def _route(logits):
    shape = logits.shape
    lane = lax.broadcasted_iota(jnp.int32, shape, 1)
    big = jnp.full(shape, 4 * LANES, jnp.int32)
    neg = jnp.full(shape, NEG, F32)
    zero = jnp.zeros(shape, F32)
    is_g = lane < N_GROUPS
    gl = jnp.where(is_g, logits, neg)
    gmax = jnp.max(gl, axis=-1, keepdims=True)
    gsum = jnp.sum(jnp.where(is_g, jnp.exp(gl - gmax), zero), axis=-1, keepdims=True)
    g_w = 1.0 / gsum
    g_i = jnp.min(jnp.where(is_g & (gl == gmax), lane, big), axis=-1, keepdims=True)
    e_lane = lane - N_GROUPS
    in_grp = (e_lane >= 0) & (e_lane < N_EXPERTS) & ((e_lane // EXPERTS_PER_GROUP) == g_i)
    el = jnp.where(in_grp, logits, neg)
    emax = jnp.max(el, axis=-1, keepdims=True)
    eexp = jnp.where(in_grp, jnp.exp(el - emax), zero)
    ep = eexp / jnp.sum(eexp, axis=-1, keepdims=True)
    minus = jnp.full(shape, -1.0, F32)
    ep1 = jnp.where(in_grp, ep, minus)
    p1 = jnp.max(ep1, axis=-1, keepdims=True)
    i1 = jnp.min(jnp.where(ep1 == p1, lane, big), axis=-1, keepdims=True)
    ep2 = jnp.where(lane == i1, minus, ep1)
    p2 = jnp.max(ep2, axis=-1, keepdims=True)
    i2 = jnp.min(jnp.where(ep2 == p2, lane, big), axis=-1, keepdims=True)
    den = p1 + p2
    cmb = g_w * (jnp.where(lane == i1, p1 / den, zero) + jnp.where(lane == i2, p2 / den, zero))
    return cmb, jnp.where(lane == g_i, 1.0, 0.0)


def _split_bf16(x):
    hi = x.astype(BF16)
    return hi, (x - hi.astype(F32)).astype(BF16)


def _moe_kernel(u_ref, lg_ref, h1_ref, tri_ref, wg_ref, wu_ref, wd_ref, fw_ref, o_ref,
                xs_scr, ys_scr, c8_scr, pos_scr, meta_ref):
    g = pl.program_id(1)
    tm = MOE_TM

    @pl.when(g == 0)
    def _():
        cmb, gsel = _route(lg_ref[...])
        lane1 = lax.broadcasted_iota(jnp.int32, (1, LANES), 1)
        rank = jnp.dot(tri_ref[...], gsel.astype(BF16), preferred_element_type=F32)
        cnt = jnp.sum(gsel, axis=0, keepdims=True)
        nch = jnp.floor((cnt + (MOE_CHUNK - 1.0)) * (1.0 / MOE_CHUNK))
        n0, n1, n2 = nch[:, 0:1], nch[:, 1:2], nch[:, 2:3]
        base = jnp.where(lane1 == 1, n0, jnp.where(lane1 == 2, n0 + n1, jnp.where(lane1 == 3, n0 + n1 + n2, 0.0)))
        for gi in range(N_GROUPS):
            meta_ref[gi] = jnp.sum(jnp.where(lane1 == gi, nch, 0.0)).astype(jnp.int32)
            meta_ref[N_GROUPS + gi] = jnp.sum(jnp.where(lane1 == gi, base, 0.0)).astype(jnp.int32)
        pos = jnp.sum(gsel * (rank + base * MOE_CHUNK), axis=-1, keepdims=True)
        pos_b = jnp.broadcast_to(pos, (tm, LANES))
        pos_scr[...] = pos_b
        pos_row = pos_b.T[0:1, :]
        rowi = lax.broadcasted_iota(jnp.int32, (MOE_CAP, tm), 0).astype(F32)
        perm = jnp.where(rowi == pos_row, 1.0, 0.0).astype(BF16)
        xs_scr[...] = jnp.dot(perm, u_ref[...], preferred_element_type=F32).astype(BF16)
        lrow = lax.broadcasted_iota(jnp.int32, (LANES, LANES), 0) - N_GROUPS
        lcol = lax.broadcasted_iota(jnp.int32, (LANES, LANES), 1)
        fold = jnp.where((lrow >= 0) & (lrow < N_EXPERTS) & ((lrow % EXPERTS_PER_GROUP) == lcol), 1.0, 0.0).astype(BF16)
        c_hi, c_lo = _split_bf16(cmb)
        c8 = jnp.dot(c_hi, fold, preferred_element_type=F32) + jnp.dot(c_lo, fold, preferred_element_type=F32)
        c8_hi, c8_lo = _split_bf16(c8)
        c8_scr[...] = (jnp.dot(perm, c8_hi, preferred_element_type=F32)
                       + jnp.dot(perm, c8_lo, preferred_element_type=F32))
        ys_scr[...] = jnp.zeros(ys_scr.shape, BF16)

    base_g = meta_ref[N_GROUPS + g]

    def chunk(jc, carry):
        r0 = pl.multiple_of((base_g + jc) * MOE_CHUNK, MOE_CHUNK)
        xc = xs_scr[pl.ds(r0, MOE_CHUNK), :]
        cw = c8_scr[pl.ds(r0, MOE_CHUNK), :]
        a = jnp.dot(xc, wg_ref[0], preferred_element_type=F32)
        b = jnp.dot(xc, wu_ref[0], preferred_element_type=F32)
        parts = []
        for e in range(EXPERTS_PER_GROUP):
            sl = slice(e * D_EXPERT, (e + 1) * D_EXPERT)
            parts.append((_silu(a[:, sl]) * b[:, sl] * cw[:, e:e + 1]).astype(BF16))
        hh = jnp.concatenate(parts, axis=1)
        ys_scr[pl.ds(r0, MOE_CHUNK), :] = jnp.dot(hh, wd_ref[0], preferred_element_type=F32).astype(BF16)
        return carry

    lax.fori_loop(0, meta_ref[g], chunk, 0)

    @pl.when(g == N_GROUPS - 1)
    def _():
        pos_w = jnp.concatenate([pos_scr[...]] * (MOE_CAP // LANES), axis=1)
        lane = lax.broadcasted_iota(jnp.int32, (tm, MOE_CAP), 1).astype(F32)
        unperm = jnp.where(lane == pos_w, 1.0, 0.0).astype(BF16)
        hf = h1_ref[...] + jnp.dot(unperm, ys_scr[...], preferred_element_type=F32)
        o_ref[...] = hf * lax.rsqrt(jnp.mean(hf * hf, axis=-1, keepdims=True) + NORM_EPS) * fw_ref[...]


def _moe(u2, logits, h1, tri, w_gate, w_up, w_down, final_w):
    seq = u2.shape[0]
    gw = EXPERTS_PER_GROUP * D_EXPERT
    once = pl.Buffered(1)
    row = lambda w, **kw: pl.BlockSpec((MOE_TM, w), lambda m, g: (m, 0), **kw)
    return pl.pallas_call(
        _moe_kernel,
        grid=(seq // MOE_TM, N_GROUPS),
        in_specs=[
            row(D_MODEL), row(LANES), row(D_MODEL, pipeline_mode=once),
            pl.BlockSpec((MOE_TM, MOE_TM), lambda m, g: (0, 0), pipeline_mode=once),
            pl.BlockSpec((1, D_MODEL, gw), lambda m, g: (g, 0, 0)),
            pl.BlockSpec((1, D_MODEL, gw), lambda m, g: (g, 0, 0)),
            pl.BlockSpec((1, gw, D_MODEL), lambda m, g: (g, 0, 0)),
            pl.BlockSpec((1, D_MODEL), lambda m, g: (0, 0)),
        ],
        out_specs=row(D_MODEL, pipeline_mode=once),
        out_shape=jax.ShapeDtypeStruct((seq, D_MODEL), F32),
        scratch_shapes=[pltpu.VMEM((MOE_CAP, D_MODEL), BF16), pltpu.VMEM((MOE_CAP, D_MODEL), BF16),
                        pltpu.VMEM((MOE_CAP, LANES), F32), pltpu.VMEM((MOE_TM, LANES), F32),
                        pltpu.SMEM((2 * N_GROUPS,), jnp.int32)],
        compiler_params=pltpu.CompilerParams(dimension_semantics=("parallel", "arbitrary"),
                                             vmem_limit_bytes=MOE_VMEM_LIMIT),
        name="moe_final",
    )(u2, logits, h1, tri, w_gate, w_up, w_down, final_w)


def kernel(x, meta_tokens, norm1_w, w_in, lambda_q1, lambda_k1, lambda_q2, lambda_k2, da_subln_w, dn_conv_w,
           dn_A_log, dn_dt_bias, dn_norm_w, w_branch_attn, w_branch_dn, w_out, norm2_w, router_group_w,
           router_group_b, router_expert_w, router_expert_b, expert_w_gate, expert_w_up, expert_w_down,
           final_norm_w):
    assert x.shape[0] == 1 and x.shape[2] == D_MODEL and x.shape[1] % ATT_TQ == 0
    assert w_in.shape[0] == 1, "single layer: the block output at the meta rows is unused"
    seq = x.shape[1]
    lt = TILE + seq
    assert lt <= 256 * LANES, "key positions must split into two exact bf16 terms"
    first = TILE - N_META

    hp = jnp.concatenate([jnp.zeros((first, D_MODEL), F32), meta_tokens.astype(F32), x[0]], axis=0)

    w = w_in[0]
    d = D_MODEL
    w_big = jnp.concatenate([w[:, 3 * d:6 * d], w[:, d:2 * d], w[:, 6 * d:7 * d], w[:, 7 * d + 2 * N_HEADS:]],
                            axis=1).astype(BF16)
    w_qv_t = jnp.concatenate([w[:, :d] * DA_HEAD_DIM ** -0.5, w[:, 2 * d:3 * d]], axis=1).T.astype(BF16)
    w_small = jnp.pad(w[:, 7 * d:7 * d + 2 * N_HEADS], ((0, 0), (0, LANES - 2 * N_HEADS)))
    proj, qvt, small = _inproj(hp, norm1_w, w_big, w_qv_t, w_small)

    pos = jnp.arange(lt, dtype=jnp.int32)
    kpos = jnp.stack([(pos >> 7) << 7, pos & 127, jnp.ones_like(pos)], axis=1).astype(BF16)
    kpos = jnp.pad(kpos, ((0, 0), (0, LANES - 3)))
    slopes = jnp.broadcast_to(jnp.asarray(ALIBI_SLOPES, F32)[:, None], (N_HEADS, LANES))
    lam_tab = jnp.pad(jnp.concatenate([lambda_q1, lambda_k1, lambda_q2, lambda_k2], axis=0),
                      ((0, 4), (0, LANES - DA_HEAD_DIM)))
    y_a = _attention(qvt, proj, kpos, slopes, lam_tab, da_subln_w, seq, first)

    pad16 = lambda t, off: jnp.pad(t, ((0, 0), (off, LANES - N_HEADS - off)))
    qn, kn, vc, beta_b, g_b = _dn_prep(proj, small, dn_conv_w[0], pad16(dn_A_log, N_HEADS),
                                       pad16(dn_dt_bias, N_HEADS), first, lt)
    y_d = _dn_chunks(qn, kn, vc, beta_b, g_b, proj, dn_norm_w)

    w_r = jnp.pad(jnp.concatenate([router_group_w[0], router_expert_w[0]], axis=1),
                  ((0, 0), (0, LANES - N_GROUPS - N_EXPERTS)))
    b_r = jnp.pad(jnp.concatenate([router_group_b, router_expert_b], axis=1),
                  ((0, 0), (0, LANES - N_GROUPS - N_EXPERTS)))
    h1, u2, logits = _mix(y_a, y_d, proj, hp, w_branch_attn[0].astype(BF16), w_branch_dn[0].astype(BF16),
                          w_out[0].astype(BF16), norm2_w, w_r, b_r)

    by_group = lambda t: (t.astype(BF16).reshape(N_GROUPS, EXPERTS_PER_GROUP, D_MODEL, D_EXPERT)
                          .transpose(0, 2, 1, 3).reshape(N_GROUPS, D_MODEL, EXPERTS_PER_GROUP * D_EXPERT))
    w_down = expert_w_down[0].astype(BF16).reshape(N_GROUPS, EXPERTS_PER_GROUP * D_EXPERT, D_MODEL)
    tri = jnp.tril(jnp.ones((MOE_TM, MOE_TM), BF16), -1)
    out = _moe(u2, logits, h1, tri, by_group(expert_w_gate[0]), by_group(expert_w_up[0]), w_down,
               final_norm_w[None, :])
    return out[None]
```

```python
import functools
import math

import jax
import jax.numpy as jnp
from jax import lax
from jax.experimental import pallas as pl
from jax.experimental.pallas import tpu as pltpu

F32 = jnp.float32
BF16 = jnp.bfloat16
HIGHEST = lax.Precision.HIGHEST

D_MODEL = 1024
N_META = 16
NORM_EPS = 1e-6
N_HEADS = 8
HEAD_W = 128
DA_HEAD_DIM = 64
DN_CONV = 4
DN_CHUNK = 64
N_GROUPS = 4
EXPERTS_PER_GROUP = 8
N_EXPERTS = 32
D_EXPERT = 256
LAMBDA_INIT = 0.8 - 0.6 * math.exp(-0.3 * 0)
ALIBI_SLOPES = tuple(2.0 ** (-8.0 * (h + 1) / N_HEADS) for h in range(N_HEADS))

TILE = 512
ATT_TQ = 1024
Q_CHAIN = 256
CHAIN_LAG = 8
V_ROWS = 144
DN_ROWS = 256
DN_HEADS_PER_STEP = 8
LANES = 128
NEG = -1e30
VMEM_LIMIT = 56 * 1024 * 1024
MOE_TM = 1024
MOE_CHUNK = 128
MOE_CAP = MOE_TM + N_GROUPS * MOE_CHUNK
MOE_VMEM_LIMIT = 60 * 1024 * 1024

CB_DQKV, CB_AK, CB_DG, CB_GA, CB_GD = 0, 24, 32, 40, 48
PROJ_COLS = 56 * 128
PROJ_TN = PROJ_COLS // 2


def _cparams(sem):
    return pltpu.CompilerParams(dimension_semantics=sem, vmem_limit_bytes=VMEM_LIMIT)


def _mm(a, b):
    return jnp.dot(a.astype(BF16), b.astype(BF16), preferred_element_type=F32)


def _mm_nt(a, b):
    return lax.dot_general(a.astype(BF16), b.astype(BF16), (((1,), (1,)), ((), ())),
                           preferred_element_type=F32)


def _mm_tn(a, b):
    return lax.dot_general(a.astype(BF16), b.astype(BF16), (((0,), (0,)), ((), ())),
                           preferred_element_type=F32)


def _mm_f32(a, b):
    return jnp.dot(a, b, precision=HIGHEST, preferred_element_type=F32)


def _silu(x):
    return x * (1.0 / (1.0 + jnp.exp(-x)))


def _sigmoid(x):
    return 1.0 / (1.0 + jnp.exp(-x))


def _inproj_kernel(x_ref, meta_ref, nw_ref, w_ref, wt_ref, ws_ref, o_ref, ot_ref, os_ref, u_scr):
    @pl.when(pl.program_id(1) == 0)
    def _():
        x = jnp.where(pl.program_id(0) == 0, meta_ref[...], x_ref[...])
        u = x * lax.rsqrt(jnp.mean(x * x, axis=-1, keepdims=True) + NORM_EPS) * nw_ref[...]
        ub = u.astype(BF16)
        u_scr[...] = ub
        os_ref[...] = _mm_f32(u, ws_ref[...])
        ot_ref[0] = lax.dot_general(wt_ref[...], ub, (((1,), (1,)), ((), ())),
                                    preferred_element_type=F32).astype(BF16)

    o_ref[...] = jnp.dot(u_scr[...], w_ref[...], preferred_element_type=F32).astype(BF16)


def _inproj(x2d, meta_tile, norm_w, w_big, w_qv_t, w_small):
    lt = TILE + x2d.shape[0]
    n_t = lt // TILE
    return pl.pallas_call(
        _inproj_kernel,
        grid=(n_t, PROJ_COLS // PROJ_TN),
        in_specs=[
            pl.BlockSpec((TILE, D_MODEL), lambda m, n: (jnp.maximum(m - 1, 0), 0)),
            pl.BlockSpec((TILE, D_MODEL), lambda m, n: (0, 0)),
            pl.BlockSpec((1, D_MODEL), lambda m, n: (0, 0)),
            pl.BlockSpec((D_MODEL, PROJ_TN), lambda m, n: (0, n)),
            pl.BlockSpec((2 * D_MODEL, D_MODEL), lambda m, n: (0, 0)),
            pl.BlockSpec((D_MODEL, LANES), lambda m, n: (0, 0)),
        ],
        out_specs=[
            pl.BlockSpec((TILE, PROJ_TN), lambda m, n: (m, n)),
            pl.BlockSpec((1, 2 * D_MODEL, TILE), lambda m, n: (m, 0, 0)),
            pl.BlockSpec((TILE, LANES), lambda m, n: (m, 0)),
        ],
        out_shape=[
            jax.ShapeDtypeStruct((lt, PROJ_COLS), BF16),
            jax.ShapeDtypeStruct((n_t, 2 * D_MODEL, TILE), BF16),
            jax.ShapeDtypeStruct((lt, LANES), F32),
        ],
        scratch_shapes=[pltpu.VMEM((TILE, D_MODEL), BF16)],
        compiler_params=_cparams(("parallel", "arbitrary")),
        name="inproj",
    )(x2d, meta_tile, norm_w, w_big, w_qv_t, w_small)


def _attn_kernel(*refs, tq, head0, bf16_exp, first_key):
    tpq = tq // TILE
    q_refs = refs[:tpq]
    k_ref, kpos_ref, vt_ref, slope_ref, lam_ref, sw_ref, o_ref, m_ref, a_ref, s_ref = refs[tpq:]
    h = pl.program_id(0) + head0
    j = pl.program_id(1)
    tk = TILE
    n_q = tq // Q_CHAIN

    qt = jnp.concatenate([r[0] for r in q_refs], axis=1)
    slope = slope_ref[pl.ds(h, 1), :][:, :1]
    qbase = TILE + j * tq
    rowi = lax.broadcasted_iota(jnp.int32, (HEAD_W, tq), 0)
    brow = jnp.where(rowi < 2, slope, jnp.where(rowi == 2, -(slope * qbase.astype(F32)), 0.0)).astype(BF16)
    z64 = jnp.zeros((DA_HEAD_DIM, tq), BF16)
    w1 = jnp.concatenate([qt[:DA_HEAD_DIM], z64, brow], axis=0)
    w2 = jnp.concatenate([z64, qt[DA_HEAD_DIM:], brow], axis=0)
    chains = [(w, hf) for w in (w1, w2) for hf in range(n_q)]
    ones_rows = jnp.where(lax.broadcasted_iota(jnp.int32, (V_ROWS - HEAD_W, tk), 0) == 0, 1.0, 0.0).astype(BF16)

    m_ref[...] = jnp.full(m_ref.shape, NEG, F32)
    a_ref[...] = jnp.zeros(a_ref.shape, F32)

    def tile(ki, modes, lo=0, rows=tk):
        start = pl.multiple_of(ki * tk, tk) + lo
        kt = jnp.concatenate([k_ref[pl.ds(start, rows), :], kpos_ref[pl.ds(start, rows), :]], axis=1)
        vt = jnp.concatenate([vt_ref[ki][:, lo:lo + rows], ones_rows[:, :rows]], axis=0)
        key = start + lax.broadcasted_iota(jnp.int32, (rows, Q_CHAIN), 0)
        lane = lax.broadcasted_iota(jnp.int32, (rows, Q_CHAIN), 1)
        live = [c for c, (_, hf) in enumerate(chains) if modes[hf] != "skip"]
        tile_max = {}
        for c in live:
            w, hf = chains[c]
            st = jnp.dot(kt, w[:, hf * Q_CHAIN:(hf + 1) * Q_CHAIN], preferred_element_type=F32)
            if modes[hf] == "front":
                st = jnp.where(key >= first_key, st, NEG)
            elif modes[hf] == "causal":
                st = jnp.where(key <= lane + (qbase + hf * Q_CHAIN), st, NEG)
            s_ref[c, :rows] = st
            tile_max[c] = jnp.max(st, axis=0, keepdims=True)
        for c in live:
            m_old = m_ref[c]
            m_new = jnp.maximum(m_old, tile_max[c])
            alpha = jnp.exp(m_old - m_new)
            if bf16_exp:
                p = jnp.exp((s_ref[c, :rows] - m_new).astype(BF16))
            else:
                p = jnp.exp(s_ref[c, :rows] - m_new).astype(BF16)
            a_ref[c] = alpha * a_ref[c] + jnp.dot(vt, p, preferred_element_type=F32)
            m_ref[c] = m_new

    tile(0, ["front"] * n_q, lo=tk - LANES, rows=LANES)

    def body(ki, carry):
        tile(ki, ["none"] * n_q)
        return carry

    lax.fori_loop(1, tpq * j + 1, body, 0)
    for d in range(tpq):
        modes = ["skip" if (hf + 1) * Q_CHAIN <= d * tk else "none" if hf * Q_CHAIN >= (d + 1) * tk else "causal"
                 for hf in range(n_q)]
        tile(tpq * j + 1 + d, modes)

    lt = lam_ref[...]
    lam = (jnp.exp(jnp.sum(lt[0:1] * lt[1:2], axis=-1, keepdims=True))
           - jnp.exp(jnp.sum(lt[2:3] * lt[3:4], axis=-1, keepdims=True)) + LAMBDA_INIT)
    for hf in range(n_q):
        a1 = a_ref[hf]
        a2 = a_ref[n_q + hf]
        ot = a1[:HEAD_W] / a1[HEAD_W:HEAD_W + 1] - lam * (a2[:HEAD_W] / a2[HEAD_W:HEAD_W + 1])
        o = ot.T
        o = o * lax.rsqrt(jnp.mean(o * o, axis=-1, keepdims=True) + NORM_EPS) * sw_ref[...]
        o_ref[hf * Q_CHAIN:(hf + 1) * Q_CHAIN, :] = (o * (1.0 - LAMBDA_INIT)).astype(BF16)


def _attention(qvt, proj, kpos, slopes, lam_tab, subln_w, seq, first_key, *, tq, head0, n_heads, bf16_exp, name):
    n_t = qvt.shape[0]
    lt = n_t * TILE
    tpq = tq // TILE
    assert tq % TILE == 0 and seq % tq == 0 and first_key >= TILE - LANES
    kern = functools.partial(_attn_kernel, tq=tq, head0=head0, bf16_exp=bf16_exp, first_key=first_key)
    n_chain = 2 * tq // Q_CHAIN
    scratch = [pltpu.VMEM((n_chain, 1, Q_CHAIN), F32), pltpu.VMEM((n_chain, V_ROWS, Q_CHAIN), F32),
               pltpu.VMEM((n_chain, TILE, Q_CHAIN), F32)]
    q_spec = lambda d: pl.BlockSpec((1, HEAD_W, TILE), lambda h, j: (1 + d + tpq * j, head0 + h, 0))
    return pl.pallas_call(
        kern,
        grid=(n_heads, seq // tq),
        in_specs=[q_spec(d) for d in range(tpq)] + [
            pl.BlockSpec((lt, HEAD_W), lambda h, j: (0, CB_AK + head0 + h)),
            pl.BlockSpec((lt, LANES), lambda h, j: (0, 0)),
            pl.BlockSpec((n_t, HEAD_W, TILE), lambda h, j: (0, N_HEADS + head0 + h, 0)),
            pl.BlockSpec((8, LANES), lambda h, j: (0, 0)),
            pl.BlockSpec((8, LANES), lambda h, j: (0, 0)),
            pl.BlockSpec((1, HEAD_W), lambda h, j: (0, 0)),
        ],
        out_specs=pl.BlockSpec((tq, HEAD_W), lambda h, j: (j, h)),
        out_shape=jax.ShapeDtypeStruct((seq, n_heads * HEAD_W), BF16),
        scratch_shapes=scratch,
        compiler_params=_cparams(("parallel", "arbitrary")),
        name=name,
    )(*([qvt] * tpq), proj, kpos, qvt, slopes, lam_tab, subln_w)


def _dn_prep_kernel(x_ref, halo_ref, sm_ref, cw_ref, al_ref, dtb_ref,
                    q_ref, k_ref, v_ref, b_ref, g_ref, xe_scr, *, first_row, last_row):
    m = pl.program_id(0)
    tm = x_ref.shape[0]
    halo = halo_ref[...].astype(F32)
    xe_scr[0:8, :] = jnp.where(m > 0, halo, jnp.zeros_like(halo))
    xe_scr[8:, :] = x_ref[...].astype(F32)
    cw = cw_ref[...]
    acc = xe_scr[8:, :] * cw[DN_CONV - 1:DN_CONV, :]
    for j in range(DN_CONV - 1):
        acc = acc + xe_scr[pl.ds(8 - (DN_CONV - 1) + j, tm), :] * cw[j:j + 1, :]
    y = _silu(acc)

    rowid = m * tm + lax.broadcasted_iota(jnp.int32, (tm, LANES), 0)
    real = (rowid >= first_row) & (rowid < last_row)
    lane = lax.broadcasted_iota(jnp.int32, (tm, LANES), 1)
    sm = sm_ref[...]
    beta_all = _sigmoid(sm)
    a = sm + dtb_ref[...]
    softplus = jnp.maximum(a, 0.0) + jnp.log(1.0 + jnp.exp(-jnp.abs(a)))
    g_all = -jnp.exp(al_ref[...]) * softplus
    zeros = jnp.zeros((tm, LANES), F32)
    width = N_HEADS * HEAD_W
    for h in range(N_HEADS):
        sl = slice(h * HEAD_W, (h + 1) * HEAD_W)
        qh = y[:, sl]
        kh = y[:, width + h * HEAD_W: width + (h + 1) * HEAD_W]
        vh = y[:, 2 * width + h * HEAD_W: 2 * width + (h + 1) * HEAD_W]
        qh = qh * lax.rsqrt(jnp.sum(qh * qh, axis=-1, keepdims=True) + 1e-6) * (HEAD_W ** -0.5)
        kh = kh * lax.rsqrt(jnp.sum(kh * kh, axis=-1, keepdims=True) + 1e-6)
        q_ref[h] = qh.astype(BF16)
        k_ref[h] = kh.astype(BF16)
        v_ref[h] = vh.astype(BF16)
        bh = jnp.sum(jnp.where(lane == h, beta_all, zeros), axis=-1, keepdims=True)
        gh = jnp.sum(jnp.where(lane == N_HEADS + h, g_all, zeros), axis=-1, keepdims=True)
        b_ref[h] = jnp.where(real, jnp.broadcast_to(bh, (tm, LANES)), zeros)
        g_ref[h] = jnp.where(real, jnp.broadcast_to(gh, (tm, LANES)), zeros)


def _dn_prep(proj, small, conv_w, a_log_row, dtb_row, first_row, last_row):
    lt = proj.shape[0]
    c = 3 * N_HEADS * HEAD_W
    kern = functools.partial(_dn_prep_kernel, first_row=first_row, last_row=last_row)
    hshape = lambda dt: jax.ShapeDtypeStruct((N_HEADS, lt, HEAD_W), dt)
    hspec = pl.BlockSpec((N_HEADS, TILE, HEAD_W), lambda m: (0, m, 0))
    return pl.pallas_call(
        kern,
        grid=(lt // TILE,),
        in_specs=[
            pl.BlockSpec((TILE, c), lambda m: (m, CB_DQKV)),
            pl.BlockSpec((8, c), lambda m: (jnp.maximum(m * (TILE // 8) - 1, 0), CB_DQKV)),
            pl.BlockSpec((TILE, LANES), lambda m: (m, 0)),
            pl.BlockSpec((DN_CONV, c), lambda m: (0, 0)),
            pl.BlockSpec((1, LANES), lambda m: (0, 0)),
            pl.BlockSpec((1, LANES), lambda m: (0, 0)),
        ],
        out_specs=[hspec, hspec, hspec, hspec, hspec],
        out_shape=[hshape(BF16), hshape(BF16), hshape(BF16), hshape(F32), hshape(F32)],
        scratch_shapes=[pltpu.VMEM((TILE + 8, c), F32)],
        compiler_params=_cparams(("parallel",)),
        name="dn_prep",
    )(proj, proj, small, conv_w, a_log_row, dtb_row)


def _dn_chunk_kernel(q_ref, k_ref, v_ref, b_ref, g_ref, gate_ref, nw_ref, o_ref, s_scr):
    @pl.when(pl.program_id(1) == 0)
    def _():
        s_scr[...] = jnp.zeros(s_scr.shape, F32)

    r = DN_ROWS
    c = DN_CHUNK
    n_chunks = r // c
    hs = range(q_ref.shape[0])

    row = lax.broadcasted_iota(jnp.int32, (r, r), 0)
    col = lax.broadcasted_iota(jnp.int32, (r, r), 1)
    same64 = (row // c) == (col // c)
    same32 = (row // 32) == (col // 32)
    same16 = (row // 16) == (col // 16)
    incl = same64 & (row >= col)
    strict = same64 & (row > col)
    off16 = same32 & jnp.logical_not(same16)
    off32 = same64 & jnp.logical_not(same32)
    zero = jnp.zeros((r, r), F32)
    incl_b = jnp.where(incl, 1.0, 0.0).astype(BF16)
    eye = jnp.where(row == col, 1.0, 0.0)

    q = [q_ref[h].astype(F32) for h in hs]
    k = [k_ref[h].astype(F32) for h in hs]
    v = [v_ref[h].astype(F32) for h in hs]
    beta = [b_ref[h] for h in hs]

    def cumsum(g):
        g_hi = g.astype(BF16)
        r1 = g - g_hi.astype(F32)
        g_mid = r1.astype(BF16)
        g_lo = (r1 - g_mid.astype(F32)).astype(BF16)
        parts = jnp.dot(incl_b, jnp.concatenate([g_hi, g_mid, g_lo], axis=1), preferred_element_type=F32)
        return (parts[:, :LANES] + parts[:, LANES:2 * LANES]) + parts[:, 2 * LANES:]

    gc = [cumsum(g_ref[h]) for h in hs]
    gc_row = [gc[h].T[0:1, :] for h in hs]
    diff = [jnp.concatenate([gc[h], gc[h]], axis=1) - gc_row[h] for h in hs]
    decay = [jnp.where(incl, jnp.exp(jnp.where(incl, diff[h], zero)), zero) for h in hs]
    gtot = [jnp.concatenate([jnp.broadcast_to(gc[h][(i + 1) * c - 1:(i + 1) * c, :], (c, LANES))
                             for i in range(n_chunks)], axis=0) for h in hs]
    egc = [jnp.exp(gc[h]) for h in hs]
    kb = [k[h] * beta[h] for h in hs]
    a = [jnp.where(strict, _mm_nt(kb[h], k[h]) * decay[h], zero) for h in hs]

    ad = [jnp.where(same16, a[h], zero) for h in hs]
    p2 = [_mm(ad[h], ad[h]) for h in hs]
    p4 = [_mm(p2[h], p2[h]) for h in hs]
    p8 = [_mm(p4[h], p4[h]) for h in hs]
    t = [_mm(eye - ad[h], eye + p2[h]) for h in hs]
    t = [_mm(t[h], eye + p4[h]) for h in hs]
    t = [_mm(t[h], eye + p8[h]) for h in hs]
    x = [_mm(t[h], jnp.where(off16, a[h], zero)) for h in hs]
    t = [t[h] - _mm(x[h], t[h]) for h in hs]
    x = [_mm(t[h], jnp.where(off32, a[h], zero)) for h in hs]
    t = [t[h] - _mm(x[h], t[h]) for h in hs]

    uw = [_mm(t[h], jnp.concatenate([v[h] * beta[h], kb[h] * egc[h]], axis=1)) for h in hs]
    qk = [jnp.where(incl, _mm_nt(q[h], k[h]) * decay[h], zero) for h in hs]
    qd = [q[h] * egc[h] for h in hs]
    kd = [k[h] * jnp.exp(gtot[h] - gc[h]) for h in hs]
    gts = [jnp.exp(gtot[h]) for h in hs]

    s = [s_scr[h] for h in hs]
    vns = [[] for _ in hs]
    outs = [[] for _ in hs]
    for i in range(n_chunks):
        rs = slice(i * c, (i + 1) * c)
        vn = [uw[h][rs, :HEAD_W] - _mm(uw[h][rs, HEAD_W:], s[h]) for h in hs]
        for h in hs:
            vns[h].append(vn[h])
        pad = [jnp.zeros(((n_chunks - 1 - i) * c, HEAD_W), F32)] * (i < n_chunks - 1)
        vn_all = [jnp.concatenate(vns[h] + pad, axis=0) for h in hs]
        o_i = [_mm(qd[h][rs], s[h]) + _mm(qk[h][rs], vn_all[h]) for h in hs]
        for h in hs:
            outs[h].append(o_i[h])
        s = [s[h] * gts[h][i * c:i * c + 1, :] + _mm_tn(kd[h][rs], vn[h]) for h in hs]
    for h in hs:
        s_scr[h] = s[h]

    for h in hs:
        o = jnp.concatenate(outs[h], axis=0)
        o = o * lax.rsqrt(jnp.mean(o * o, axis=-1, keepdims=True) + NORM_EPS) * nw_ref[...]
        gate = gate_ref[:, h * HEAD_W:(h + 1) * HEAD_W].astype(F32)
        o_ref[:, h * HEAD_W:(h + 1) * HEAD_W] = (o * _silu(gate)).astype(BF16)


def _dn_chunks(qn, kn, vc, beta_b, g_b, proj, norm_w):
    lt = qn.shape[1]
    hg = DN_HEADS_PER_STEP
    first_blk = (TILE - DN_ROWS) // DN_ROWS
    x_blk = TILE // DN_ROWS - first_blk
    hspec = pl.BlockSpec((hg, DN_ROWS, HEAD_W), lambda g, i: (g, i + first_blk, 0))
    wide = hg * HEAD_W
    return pl.pallas_call(
        _dn_chunk_kernel,
        grid=(N_HEADS // hg, lt // DN_ROWS - first_blk),
        in_specs=[
            hspec, hspec, hspec, hspec, hspec,
            pl.BlockSpec((DN_ROWS, wide), lambda g, i: (i + first_blk, CB_DG * HEAD_W // wide + g)),
            pl.BlockSpec((1, HEAD_W), lambda g, i: (0, 0)),
        ],
        out_specs=pl.BlockSpec((DN_ROWS, wide), lambda g, i: (jnp.maximum(i - x_blk, 0), g)),
        out_shape=jax.ShapeDtypeStruct((lt - TILE, N_HEADS * HEAD_W), BF16),
        scratch_shapes=[pltpu.VMEM((hg, HEAD_W, HEAD_W), F32)],
        compiler_params=_cparams(("parallel", "arbitrary")),
        name="dn_chunks",
    )(qn, kn, vc, beta_b, g_b, proj, norm_w)


def _mix_kernel(ya_ref, yd_ref, ga_ref, gd_ref, h_ref, wa_ref, wd_ref, wo_ref, nw_ref, wr_ref, br_ref,
                h1_ref, u2_ref, lg_ref):
    pa = jnp.dot(ya_ref[...], wa_ref[...], preferred_element_type=F32)
    pd = jnp.dot(yd_ref[...], wd_ref[...], preferred_element_type=F32)
    mixed = _sigmoid(ga_ref[...].astype(F32)) * pa + _sigmoid(gd_ref[...].astype(F32)) * pd
    h1 = h_ref[...] + jnp.dot(mixed.astype(BF16), wo_ref[...], preferred_element_type=F32)
    h1_ref[...] = h1
    u2 = h1 * lax.rsqrt(jnp.mean(h1 * h1, axis=-1, keepdims=True) + NORM_EPS) * nw_ref[...]
    u2_ref[...] = u2.astype(BF16)
    lg_ref[...] = _mm_f32(u2, wr_ref[...]) + br_ref[...]


def _mix(y_a, y_d, proj, x2d, w_a, w_d, w_o, norm_w, w_r, b_r):
    seq = y_a.shape[0]
    skip = (proj.shape[0] - seq) // TILE
    row = lambda: pl.BlockSpec((TILE, D_MODEL), lambda m: (m, 0))
    prow = lambda cb: pl.BlockSpec((TILE, D_MODEL), lambda m: (m + skip, cb))
    full = lambda: pl.BlockSpec((D_MODEL, D_MODEL), lambda m: (0, 0))
    return pl.pallas_call(
        _mix_kernel,
        grid=(seq // TILE,),
        in_specs=[
            row(), row(), prow(CB_GA // 8), prow(CB_GD // 8), row(), full(), full(), full(),
            pl.BlockSpec((1, D_MODEL), lambda m: (0, 0)),
            pl.BlockSpec((D_MODEL, LANES), lambda m: (0, 0)),
            pl.BlockSpec((1, LANES), lambda m: (0, 0)),
        ],
        out_specs=[row(), row(), pl.BlockSpec((TILE, LANES), lambda m: (m, 0))],
        out_shape=[
            jax.ShapeDtypeStruct((seq, D_MODEL), F32),
            jax.ShapeDtypeStruct((seq, D_MODEL), BF16),
            jax.ShapeDtypeStruct((seq, LANES), F32),
        ],
        compiler_params=_cparams(("parallel",)),
        name="branch_mix",
    )(y_a, y_d, proj, proj, x2d, w_a, w_d, w_o, norm_w, w_r, b_r)


def _route(logits):
    shape = logits.shape
    lane = lax.broadcasted_iota(jnp.int32, shape, 1)
    big = jnp.full(shape, 4 * LANES, jnp.int32)
    neg = jnp.full(shape, NEG, F32)
    zero = jnp.zeros(shape, F32)
    is_g = lane < N_GROUPS
    gl = jnp.where(is_g, logits, neg)
    gmax = jnp.max(gl, axis=-1, keepdims=True)
    gsum = jnp.sum(jnp.where(is_g, jnp.exp(gl - gmax), zero), axis=-1, keepdims=True)
    g_w = 1.0 / gsum
    g_i = jnp.min(jnp.where(is_g & (gl == gmax), lane, big), axis=-1, keepdims=True)
    e_lane = lane - N_GROUPS
    in_grp = (e_lane >= 0) & (e_lane < N_EXPERTS) & ((e_lane // EXPERTS_PER_GROUP) == g_i)
    el = jnp.where(in_grp, logits, neg)
    emax = jnp.max(el, axis=-1, keepdims=True)
    eexp = jnp.where(in_grp, jnp.exp(el - emax), zero)
    ep = eexp / jnp.sum(eexp, axis=-1, keepdims=True)
    minus = jnp.full(shape, -1.0, F32)
    ep1 = jnp.where(in_grp, ep, minus)
    p1 = jnp.max(ep1, axis=-1, keepdims=True)
    i1 = jnp.min(jnp.where(ep1 == p1, lane, big), axis=-1, keepdims=True)
    ep2 = jnp.where(lane == i1, minus, ep1)
    p2 = jnp.max(ep2, axis=-1, keepdims=True)
    i2 = jnp.min(jnp.where(ep2 == p2, lane, big), axis=-1, keepdims=True)
    den = p1 + p2
    cmb = g_w * (jnp.where(lane == i1, p1 / den, zero) + jnp.where(lane == i2, p2 / den, zero))
    return cmb, jnp.where(lane == g_i, 1.0, 0.0)


def _split_bf16(x):
    hi = x.astype(BF16)
    return hi, (x - hi.astype(F32)).astype(BF16)


def _moe_kernel(u_ref, lg_ref, h1_ref, tri_ref, wg_ref, wu_ref, wd_ref, fw_ref, o_ref,
                xs_scr, ys_scr, c8_scr, pos_scr, meta_ref):
    g = pl.program_id(1)
    tm = MOE_TM

    @pl.when(g == 0)
    def _():
        cmb, gsel = _route(lg_ref[...])
        lane1 = lax.broadcasted_iota(jnp.int32, (1, LANES), 1)
        rank = jnp.dot(tri_ref[...], gsel.astype(BF16), preferred_element_type=F32)
        cnt = jnp.sum(gsel, axis=0, keepdims=True)
        nch = jnp.floor((cnt + (MOE_CHUNK - 1.0)) * (1.0 / MOE_CHUNK))
        n0, n1, n2 = nch[:, 0:1], nch[:, 1:2], nch[:, 2:3]
        base = jnp.where(lane1 == 1, n0, jnp.where(lane1 == 2, n0 + n1, jnp.where(lane1 == 3, n0 + n1 + n2, 0.0)))
        for gi in range(N_GROUPS):
            meta_ref[gi] = jnp.sum(jnp.where(lane1 == gi, nch, 0.0)).astype(jnp.int32)
            meta_ref[N_GROUPS + gi] = jnp.sum(jnp.where(lane1 == gi, base, 0.0)).astype(jnp.int32)
        pos = jnp.sum(gsel * (rank + base * MOE_CHUNK), axis=-1, keepdims=True)
        pos_b = jnp.broadcast_to(pos, (tm, LANES))
        pos_scr[...] = pos_b
        pos_row = pos_b.T[0:1, :]
        rowi = lax.broadcasted_iota(jnp.int32, (MOE_CAP, tm), 0).astype(F32)
        perm = jnp.where(rowi == pos_row, 1.0, 0.0).astype(BF16)
        xs_scr[...] = jnp.dot(perm, u_ref[...], preferred_element_type=F32).astype(BF16)
        lrow = lax.broadcasted_iota(jnp.int32, (LANES, LANES), 0) - N_GROUPS
        lcol = lax.broadcasted_iota(jnp.int32, (LANES, LANES), 1)
        fold = jnp.where((lrow >= 0) & (lrow < N_EXPERTS) & ((lrow % EXPERTS_PER_GROUP) == lcol), 1.0, 0.0).astype(BF16)
        c_hi, c_lo = _split_bf16(cmb)
        c8 = jnp.dot(c_hi, fold, preferred_element_type=F32) + jnp.dot(c_lo, fold, preferred_element_type=F32)
        c8_hi, c8_lo = _split_bf16(c8)
        c8_scr[...] = (jnp.dot(perm, c8_hi, preferred_element_type=F32)
                       + jnp.dot(perm, c8_lo, preferred_element_type=F32))
        ys_scr[...] = jnp.zeros(ys_scr.shape, BF16)

    base_g = meta_ref[N_GROUPS + g]

    def chunk(jc, carry):
        r0 = pl.multiple_of((base_g + jc) * MOE_CHUNK, MOE_CHUNK)
        xc = xs_scr[pl.ds(r0, MOE_CHUNK), :]
        cw = c8_scr[pl.ds(r0, MOE_CHUNK), :]
        parts = []
        for e in range(EXPERTS_PER_GROUP):
            a = jnp.dot(xc, wg_ref[e], preferred_element_type=F32)
            b = jnp.dot(xc, wu_ref[e], preferred_element_type=F32)
            parts.append((_silu(a) * b * cw[:, e:e + 1]).astype(BF16))
        hh = jnp.concatenate(parts, axis=1)
        wd = wd_ref[...].reshape(EXPERTS_PER_GROUP * D_EXPERT, D_MODEL)
        ys_scr[pl.ds(r0, MOE_CHUNK), :] = jnp.dot(hh, wd, preferred_element_type=F32).astype(BF16)
        return carry

    lax.fori_loop(0, meta_ref[g], chunk, 0)

    @pl.when(g == N_GROUPS - 1)
    def _():
        pos_w = jnp.concatenate([pos_scr[...]] * (MOE_CAP // LANES), axis=1)
        lane = lax.broadcasted_iota(jnp.int32, (tm, MOE_CAP), 1).astype(F32)
        unperm = jnp.where(lane == pos_w, 1.0, 0.0).astype(BF16)
        hf = h1_ref[...] + jnp.dot(unperm, ys_scr[...], preferred_element_type=F32)
        o_ref[...] = hf * lax.rsqrt(jnp.mean(hf * hf, axis=-1, keepdims=True) + NORM_EPS) * fw_ref[...]


def _moe(u2, logits, h1, tri, w_gate, w_up, w_down, final_w):
    seq = u2.shape[0]
    once = pl.Buffered(1)
    row = lambda w, **kw: pl.BlockSpec((MOE_TM, w), lambda m, g: (m, 0), **kw)
    return pl.pallas_call(
        _moe_kernel,
        grid=(seq // MOE_TM, N_GROUPS),
        in_specs=[
            row(D_MODEL), row(LANES), row(D_MODEL, pipeline_mode=once),
            pl.BlockSpec((MOE_TM, MOE_TM), lambda m, g: (0, 0), pipeline_mode=once),
            pl.BlockSpec((EXPERTS_PER_GROUP, D_MODEL, D_EXPERT), lambda m, g: (g, 0, 0)),
            pl.BlockSpec((EXPERTS_PER_GROUP, D_MODEL, D_EXPERT), lambda m, g: (g, 0, 0)),
            pl.BlockSpec((EXPERTS_PER_GROUP, D_EXPERT, D_MODEL), lambda m, g: (g, 0, 0)),
            pl.BlockSpec((1, D_MODEL), lambda m, g: (0, 0)),
        ],
        out_specs=row(D_MODEL, pipeline_mode=once),
        out_shape=jax.ShapeDtypeStruct((seq, D_MODEL), F32),
        scratch_shapes=[pltpu.VMEM((MOE_CAP, D_MODEL), BF16), pltpu.VMEM((MOE_CAP, D_MODEL), BF16),
                        pltpu.VMEM((MOE_CAP, LANES), F32), pltpu.VMEM((MOE_TM, LANES), F32),
                        pltpu.SMEM((2 * N_GROUPS,), jnp.int32)],
        compiler_params=pltpu.CompilerParams(dimension_semantics=("parallel", "arbitrary"),
                                             vmem_limit_bytes=MOE_VMEM_LIMIT),
        name="moe_final",
    )(u2, logits, h1, tri, w_gate, w_up, w_down, final_w)


def kernel(x, meta_tokens, norm1_w, w_in, lambda_q1, lambda_k1, lambda_q2, lambda_k2, da_subln_w, dn_conv_w,
           dn_A_log, dn_dt_bias, dn_norm_w, w_branch_attn, w_branch_dn, w_out, norm2_w, router_group_w,
           router_group_b, router_expert_w, router_expert_b, expert_w_gate, expert_w_up, expert_w_down,
           final_norm_w):
    assert x.shape[0] == 1 and x.shape[2] == D_MODEL and x.shape[1] % ATT_TQ == 0
    assert w_in.shape[0] == 1, "single layer: the block output at the meta rows is unused"
    seq = x.shape[1]
    lt = TILE + seq
    assert lt <= 256 * LANES, "key positions must split into two exact bf16 terms"
    first = TILE - N_META

    x2d = x[0]
    meta_tile = jnp.concatenate([jnp.zeros((first, D_MODEL), F32), meta_tokens.astype(F32)], axis=0)

    w = w_in[0]
    d = D_MODEL
    w_big = jnp.concatenate([w[:, 3 * d:6 * d], w[:, d:2 * d], w[:, 6 * d:7 * d], w[:, 7 * d + 2 * N_HEADS:]],
                            axis=1).astype(BF16)
    w_qv_t = jnp.concatenate([w[:, :d] * DA_HEAD_DIM ** -0.5, w[:, 2 * d:3 * d]], axis=1).T.astype(BF16)
    w_small = jnp.pad(w[:, 7 * d:7 * d + 2 * N_HEADS], ((0, 0), (0, LANES - 2 * N_HEADS)))
    proj, qvt, small = _inproj(x2d, meta_tile, norm1_w, w_big, w_qv_t, w_small)

    pos = jnp.arange(lt, dtype=jnp.int32)
    kpos = jnp.stack([(pos >> 7) << 7, pos & 127, jnp.ones_like(pos)], axis=1).astype(BF16)
    kpos = jnp.pad(kpos, ((0, 0), (0, LANES - 3)))
    slopes = jnp.broadcast_to(jnp.asarray(ALIBI_SLOPES, F32)[:, None], (N_HEADS, LANES))
    lam_tab = jnp.pad(jnp.concatenate([lambda_q1, lambda_k1, lambda_q2, lambda_k2], axis=0),
                      ((0, 4), (0, LANES - DA_HEAD_DIM)))
    att = functools.partial(_attention, qvt, proj, kpos, slopes, lam_tab, da_subln_w, seq, first)
    y_a = jnp.concatenate([
        att(tq=1024, head0=0, n_heads=2, bf16_exp=False, name="diff_attn_a"),
        att(tq=2048, head0=2, n_heads=2, bf16_exp=False, name="diff_attn_b"),
        att(tq=1024, head0=4, n_heads=2, bf16_exp=True, name="diff_attn_c"),
        att(tq=2048, head0=6, n_heads=2, bf16_exp=True, name="diff_attn_d"),
    ], axis=1)

    pad16 = lambda t, off: jnp.pad(t, ((0, 0), (off, LANES - N_HEADS - off)))
    qn, kn, vc, beta_b, g_b = _dn_prep(proj, small, dn_conv_w[0], pad16(dn_A_log, N_HEADS),
                                       pad16(dn_dt_bias, N_HEADS), first, lt)
    y_d = _dn_chunks(qn, kn, vc, beta_b, g_b, proj, dn_norm_w)

    w_r = jnp.pad(jnp.concatenate([router_group_w[0], router_expert_w[0]], axis=1),
                  ((0, 0), (0, LANES - N_GROUPS - N_EXPERTS)))
    b_r = jnp.pad(jnp.concatenate([router_group_b, router_expert_b], axis=1),
                  ((0, 0), (0, LANES - N_GROUPS - N_EXPERTS)))
    h1, u2, logits = _mix(y_a, y_d, proj, x2d, w_branch_attn[0].astype(BF16), w_branch_dn[0].astype(BF16),
                          w_out[0].astype(BF16), norm2_w, w_r, b_r)

    tri = jnp.tril(jnp.ones((MOE_TM, MOE_TM), BF16), -1)
    out = _moe(u2, logits, h1, tri, expert_w_gate[0].astype(BF16), expert_w_up[0].astype(BF16),
               expert_w_down[0].astype(BF16), final_norm_w[None, :])
    return out[None]
```

```python
import functools
import math

import jax
import jax.numpy as jnp
from jax import lax
from jax.experimental import pallas as pl
from jax.experimental.pallas import tpu as pltpu

F32 = jnp.float32
BF16 = jnp.bfloat16
HIGHEST = lax.Precision.HIGHEST

D_MODEL = 1024
N_META = 16
NORM_EPS = 1e-6
N_HEADS = 8
HEAD_W = 128
DA_HEAD_DIM = 64
DN_CONV = 4
DN_CHUNK = 64
N_GROUPS = 4
EXPERTS_PER_GROUP = 8
N_EXPERTS = 32
D_EXPERT = 256
LAMBDA_INIT = 0.8 - 0.6 * math.exp(-0.3 * 0)
ALIBI_SLOPES = tuple(2.0 ** (-8.0 * (h + 1) / N_HEADS) for h in range(N_HEADS))

TILE = 512
ATT_TQ = 2048
Q_CHAIN = 256
V_ROWS = 144
DN_ROWS = 256
DN_HEADS_PER_STEP = 8
MIX_SUB = 2
LANES = 128
NEG = -1e30
VMEM_LIMIT = 56 * 1024 * 1024
MOE_TM = 1024
MOE_CHUNK = 128
MOE_CAP = MOE_TM + N_GROUPS * MOE_CHUNK
MOE_VMEM_LIMIT = 60 * 1024 * 1024

CB_DQKV, CB_AK, CB_DG, CB_GA, CB_GD = 0, 24, 32, 40, 48
PROJ_COLS = 56 * 128
PROJ_TN = PROJ_COLS // 2


def _cparams(sem):
    return pltpu.CompilerParams(dimension_semantics=sem, vmem_limit_bytes=VMEM_LIMIT)


def _mm(a, b):
    return jnp.dot(a.astype(BF16), b.astype(BF16), preferred_element_type=F32)


def _mm_nt(a, b):
    return lax.dot_general(a.astype(BF16), b.astype(BF16), (((1,), (1,)), ((), ())),
                           preferred_element_type=F32)


def _mm_tn(a, b):
    return lax.dot_general(a.astype(BF16), b.astype(BF16), (((0,), (0,)), ((), ())),
                           preferred_element_type=F32)


def _mm_f32(a, b):
    return jnp.dot(a, b, precision=HIGHEST, preferred_element_type=F32)


def _silu(x):
    return x * (1.0 / (1.0 + jnp.exp(-x)))


def _sigmoid(x):
    return 1.0 / (1.0 + jnp.exp(-x))


def _inproj_kernel(x_ref, meta_ref, nw_ref, w_ref, wt_ref, ws_ref, o_ref, ot_ref, os_ref, u_scr):
    @pl.when(pl.program_id(1) == 0)
    def _():
        x = jnp.where(pl.program_id(0) == 0, meta_ref[...], x_ref[...])
        u = x * lax.rsqrt(jnp.mean(x * x, axis=-1, keepdims=True) + NORM_EPS) * nw_ref[...]
        ub = u.astype(BF16)
        u_scr[...] = ub
        os_ref[...] = _mm_f32(u, ws_ref[...])
        ot_ref[0] = lax.dot_general(wt_ref[...], ub, (((1,), (1,)), ((), ())),
                                    preferred_element_type=F32).astype(BF16)

    o_ref[...] = jnp.dot(u_scr[...], w_ref[...], preferred_element_type=F32).astype(BF16)


def _inproj(x2d, meta_tile, norm_w, w_big, w_qv_t, w_small):
    lt = TILE + x2d.shape[0]
    n_t = lt // TILE
    return pl.pallas_call(
        _inproj_kernel,
        grid=(n_t, PROJ_COLS // PROJ_TN),
        in_specs=[
            pl.BlockSpec((TILE, D_MODEL), lambda m, n: (jnp.maximum(m - 1, 0), 0)),
            pl.BlockSpec((TILE, D_MODEL), lambda m, n: (0, 0)),
            pl.BlockSpec((1, D_MODEL), lambda m, n: (0, 0)),
            pl.BlockSpec((D_MODEL, PROJ_TN), lambda m, n: (0, n)),
            pl.BlockSpec((2 * D_MODEL, D_MODEL), lambda m, n: (0, 0)),
            pl.BlockSpec((D_MODEL, LANES), lambda m, n: (0, 0)),
        ],
        out_specs=[
            pl.BlockSpec((TILE, PROJ_TN), lambda m, n: (m, n)),
            pl.BlockSpec((1, 2 * D_MODEL, TILE), lambda m, n: (m, 0, 0)),
            pl.BlockSpec((TILE, LANES), lambda m, n: (m, 0)),
        ],
        out_shape=[
            jax.ShapeDtypeStruct((lt, PROJ_COLS), BF16),
            jax.ShapeDtypeStruct((n_t, 2 * D_MODEL, TILE), BF16),
            jax.ShapeDtypeStruct((lt, LANES), F32),
        ],
        scratch_shapes=[pltpu.VMEM((TILE, D_MODEL), BF16)],
        compiler_params=_cparams(("parallel", "arbitrary")),
        name="inproj",
    )(x2d, meta_tile, norm_w, w_big, w_qv_t, w_small)


def _attn_kernel(*refs, tq, first_key):
    tpq = tq // TILE
    q_refs = refs[:tpq]
    k_ref, kpos_ref, vt_ref, slope_ref, lam_ref, sw_ref, o_ref, m_ref, a_ref, s_ref = refs[tpq:]
    h = pl.program_id(0)
    j = pl.program_id(1)
    tk = TILE
    n_q = tq // Q_CHAIN

    qt = jnp.concatenate([r[0] for r in q_refs], axis=1)
    slope = slope_ref[pl.ds(h, 1), :][:, :1]
    qbase = TILE + j * tq
    rowi = lax.broadcasted_iota(jnp.int32, (HEAD_W, tq), 0)
    brow = jnp.where(rowi < 2, slope, jnp.where(rowi == 2, -(slope * qbase.astype(F32)), 0.0)).astype(BF16)
    z64 = jnp.zeros((DA_HEAD_DIM, tq), BF16)
    w1 = jnp.concatenate([qt[:DA_HEAD_DIM], z64, brow], axis=0)
    w2 = jnp.concatenate([z64, qt[DA_HEAD_DIM:], brow], axis=0)
    chains = [(w, hf) for w in (w1, w2) for hf in range(n_q)]
    ones_rows = jnp.where(lax.broadcasted_iota(jnp.int32, (V_ROWS - HEAD_W, tk), 0) == 0, 1.0, 0.0).astype(BF16)

    m_ref[...] = jnp.full(m_ref.shape, NEG, F32)
    a_ref[...] = jnp.zeros(a_ref.shape, F32)

    def tile(ki, modes, lo=0, rows=tk):
        start = pl.multiple_of(ki * tk, tk) + lo
        kt = jnp.concatenate([k_ref[pl.ds(start, rows), :], kpos_ref[pl.ds(start, rows), :]], axis=1)
        vt = jnp.concatenate([vt_ref[ki][:, lo:lo + rows], ones_rows[:, :rows]], axis=0)
        key = start + lax.broadcasted_iota(jnp.int32, (rows, Q_CHAIN), 0)
        lane = lax.broadcasted_iota(jnp.int32, (rows, Q_CHAIN), 1)
        live = [c for c, (_, hf) in enumerate(chains) if modes[hf] != "skip"]
        tile_max = {}
        for c in live:
            w, hf = chains[c]
            st = jnp.dot(kt, w[:, hf * Q_CHAIN:(hf + 1) * Q_CHAIN], preferred_element_type=F32)
            if modes[hf] == "front":
                st = jnp.where(key >= first_key, st, NEG)
            elif modes[hf] == "causal":
                st = jnp.where(key <= lane + (qbase + hf * Q_CHAIN), st, NEG)
            s_ref[c, :rows] = st
            tile_max[c] = jnp.max(st, axis=0, keepdims=True)
        for c in live:
            m_old = m_ref[c]
            m_new = jnp.maximum(m_old, tile_max[c])
            alpha = jnp.exp(m_old - m_new)
            p = jnp.exp(s_ref[c, :rows] - m_new).astype(BF16)
            a_ref[c] = alpha * a_ref[c] + jnp.dot(vt, p, preferred_element_type=F32)
            m_ref[c] = m_new

    tile(0, ["front"] * n_q, lo=tk - LANES, rows=LANES)

    def body(ki, carry):
        tile(ki, ["none"] * n_q)
        return carry

    lax.fori_loop(1, tpq * j + 1, body, 0)
    for d in range(tpq):
        modes = ["skip" if (hf + 1) * Q_CHAIN <= d * tk else "none" if hf * Q_CHAIN >= (d + 1) * tk else "causal"
                 for hf in range(n_q)]
        tile(tpq * j + 1 + d, modes)

    lt = lam_ref[...]
    lam = (jnp.exp(jnp.sum(lt[0:1] * lt[1:2], axis=-1, keepdims=True))
           - jnp.exp(jnp.sum(lt[2:3] * lt[3:4], axis=-1, keepdims=True)) + LAMBDA_INIT)
    for hf in range(n_q):
        a1 = a_ref[hf]
        a2 = a_ref[n_q + hf]
        ot = a1[:HEAD_W] / a1[HEAD_W:HEAD_W + 1] - lam * (a2[:HEAD_W] / a2[HEAD_W:HEAD_W + 1])
        o = ot.T
        o = o * lax.rsqrt(jnp.mean(o * o, axis=-1, keepdims=True) + NORM_EPS) * sw_ref[...]
        o_ref[hf * Q_CHAIN:(hf + 1) * Q_CHAIN, :] = (o * (1.0 - LAMBDA_INIT)).astype(BF16)


def _attention(qvt, proj, kpos, slopes, lam_tab, subln_w, seq, first_key):
    n_t = qvt.shape[0]
    lt = n_t * TILE
    tq = ATT_TQ
    tpq = tq // TILE
    assert tq % TILE == 0 and seq % tq == 0 and first_key >= TILE - LANES
    kern = functools.partial(_attn_kernel, tq=tq, first_key=first_key)
    n_chain = 2 * tq // Q_CHAIN
    scratch = [pltpu.VMEM((n_chain, 1, Q_CHAIN), F32), pltpu.VMEM((n_chain, V_ROWS, Q_CHAIN), F32),
               pltpu.VMEM((n_chain, TILE, Q_CHAIN), F32)]
    q_spec = lambda d: pl.BlockSpec((1, HEAD_W, TILE), lambda h, j: (1 + d + tpq * j, h, 0))
    once = pl.Buffered(1)
    return pl.pallas_call(
        kern,
        grid=(N_HEADS, seq // tq),
        in_specs=[q_spec(d) for d in range(tpq)] + [
            pl.BlockSpec((lt, HEAD_W), lambda h, j: (0, CB_AK + h), pipeline_mode=once),
            pl.BlockSpec((lt, LANES), lambda h, j: (0, 0), pipeline_mode=once),
            pl.BlockSpec((n_t, HEAD_W, TILE), lambda h, j: (0, N_HEADS + h, 0), pipeline_mode=once),
            pl.BlockSpec((8, LANES), lambda h, j: (0, 0)),
            pl.BlockSpec((8, LANES), lambda h, j: (0, 0)),
            pl.BlockSpec((1, HEAD_W), lambda h, j: (0, 0)),
        ],
        out_specs=pl.BlockSpec((tq, HEAD_W), lambda h, j: (j, h)),
        out_shape=jax.ShapeDtypeStruct((seq, N_HEADS * HEAD_W), BF16),
        scratch_shapes=scratch,
        compiler_params=_cparams(("parallel", "arbitrary")),
        name="diff_attn",
    )(*([qvt] * tpq), proj, kpos, qvt, slopes, lam_tab, subln_w)


def _dn_prep_kernel(x_ref, halo_ref, sm_ref, cw_ref, al_ref, dtb_ref,
                    q_ref, k_ref, v_ref, b_ref, g_ref, xe_scr, *, first_row, last_row):
    m = pl.program_id(0)
    tm = x_ref.shape[0]
    halo = halo_ref[...].astype(F32)
    xe_scr[0:8, :] = jnp.where(m > 0, halo, jnp.zeros_like(halo))
    xe_scr[8:, :] = x_ref[...].astype(F32)
    cw = cw_ref[...]
    acc = xe_scr[8:, :] * cw[DN_CONV - 1:DN_CONV, :]
    for j in range(DN_CONV - 1):
        acc = acc + xe_scr[pl.ds(8 - (DN_CONV - 1) + j, tm), :] * cw[j:j + 1, :]
    y = _silu(acc)

    rowid = m * tm + lax.broadcasted_iota(jnp.int32, (tm, LANES), 0)
    real = (rowid >= first_row) & (rowid < last_row)
    lane = lax.broadcasted_iota(jnp.int32, (tm, LANES), 1)
    sm = sm_ref[...]
    beta_all = _sigmoid(sm)
    a = sm + dtb_ref[...]
    softplus = jnp.maximum(a, 0.0) + jnp.log(1.0 + jnp.exp(-jnp.abs(a)))
    g_all = -jnp.exp(al_ref[...]) * softplus
    zeros = jnp.zeros((tm, LANES), F32)
    width = N_HEADS * HEAD_W
    for h in range(N_HEADS):
        sl = slice(h * HEAD_W, (h + 1) * HEAD_W)
        qh = y[:, sl]
        kh = y[:, width + h * HEAD_W: width + (h + 1) * HEAD_W]
        vh = y[:, 2 * width + h * HEAD_W: 2 * width + (h + 1) * HEAD_W]
        qh = qh * lax.rsqrt(jnp.sum(qh * qh, axis=-1, keepdims=True) + 1e-6) * (HEAD_W ** -0.5)
        kh = kh * lax.rsqrt(jnp.sum(kh * kh, axis=-1, keepdims=True) + 1e-6)
        q_ref[h] = qh.astype(BF16)
        k_ref[h] = kh.astype(BF16)
        v_ref[h] = vh.astype(BF16)
        bh = jnp.sum(jnp.where(lane == h, beta_all, zeros), axis=-1, keepdims=True)
        gh = jnp.sum(jnp.where(lane == N_HEADS + h, g_all, zeros), axis=-1, keepdims=True)
        b_ref[h] = jnp.where(real, jnp.broadcast_to(bh, (tm, LANES)), zeros)
        g_ref[h] = jnp.where(real, jnp.broadcast_to(gh, (tm, LANES)), zeros)


def _dn_prep(proj, small, conv_w, a_log_row, dtb_row, first_row, last_row):
    lt = proj.shape[0]
    c = 3 * N_HEADS * HEAD_W
    kern = functools.partial(_dn_prep_kernel, first_row=first_row, last_row=last_row)
    hshape = lambda dt: jax.ShapeDtypeStruct((N_HEADS, lt, HEAD_W), dt)
    hspec = pl.BlockSpec((N_HEADS, TILE, HEAD_W), lambda m: (0, m, 0))
    return pl.pallas_call(
        kern,
        grid=(lt // TILE,),
        in_specs=[
            pl.BlockSpec((TILE, c), lambda m: (m, CB_DQKV)),
            pl.BlockSpec((8, c), lambda m: (jnp.maximum(m * (TILE // 8) - 1, 0), CB_DQKV)),
            pl.BlockSpec((TILE, LANES), lambda m: (m, 0)),
            pl.BlockSpec((DN_CONV, c), lambda m: (0, 0)),
            pl.BlockSpec((1, LANES), lambda m: (0, 0)),
            pl.BlockSpec((1, LANES), lambda m: (0, 0)),
        ],
        out_specs=[hspec, hspec, hspec, hspec, hspec],
        out_shape=[hshape(BF16), hshape(BF16), hshape(BF16), hshape(F32), hshape(F32)],
        scratch_shapes=[pltpu.VMEM((TILE + 8, c), F32)],
        compiler_params=_cparams(("parallel",)),
        name="dn_prep",
    )(proj, proj, small, conv_w, a_log_row, dtb_row)


def _dn_chunk_kernel(q_ref, k_ref, v_ref, b_ref, g_ref, gate_ref, nw_ref, o_ref, s_scr):
    @pl.when(pl.program_id(1) == 0)
    def _():
        s_scr[...] = jnp.zeros(s_scr.shape, F32)

    r = DN_ROWS
    c = DN_CHUNK
    n_chunks = r // c
    hs = range(q_ref.shape[0])

    row = lax.broadcasted_iota(jnp.int32, (r, r), 0)
    col = lax.broadcasted_iota(jnp.int32, (r, r), 1)
    same64 = (row // c) == (col // c)
    same32 = (row // 32) == (col // 32)
    same16 = (row // 16) == (col // 16)
    incl = same64 & (row >= col)
    strict = same64 & (row > col)
    off16 = same32 & jnp.logical_not(same16)
    off32 = same64 & jnp.logical_not(same32)
    zero = jnp.zeros((r, r), F32)
    incl_b = jnp.where(incl, 1.0, 0.0).astype(BF16)
    eye = jnp.where(row == col, 1.0, 0.0)

    q = [q_ref[h].astype(F32) for h in hs]
    k = [k_ref[h].astype(F32) for h in hs]
    v = [v_ref[h].astype(F32) for h in hs]
    beta = [b_ref[h] for h in hs]

    def cumsum(g):
        g_hi = g.astype(BF16)
        r1 = g - g_hi.astype(F32)
        g_mid = r1.astype(BF16)
        g_lo = (r1 - g_mid.astype(F32)).astype(BF16)
        parts = jnp.dot(incl_b, jnp.concatenate([g_hi, g_mid, g_lo], axis=1), preferred_element_type=F32)
        return (parts[:, :LANES] + parts[:, LANES:2 * LANES]) + parts[:, 2 * LANES:]

    gc = [cumsum(g_ref[h]) for h in hs]
    gc_row = [gc[h].T[0:1, :] for h in hs]
    diff = [jnp.concatenate([gc[h], gc[h]], axis=1) - gc_row[h] for h in hs]
    decay = [jnp.where(incl, jnp.exp(jnp.where(incl, diff[h], zero)), zero) for h in hs]
    gtot = [jnp.concatenate([jnp.broadcast_to(gc[h][(i + 1) * c - 1:(i + 1) * c, :], (c, LANES))
                             for i in range(n_chunks)], axis=0) for h in hs]
    egc = [jnp.exp(gc[h]) for h in hs]
    kb = [k[h] * beta[h] for h in hs]
    a = [jnp.where(strict, _mm_nt(kb[h], k[h]) * decay[h], zero) for h in hs]

    ad = [jnp.where(same16, a[h], zero) for h in hs]
    p2 = [_mm(ad[h], ad[h]) for h in hs]
    p4 = [_mm(p2[h], p2[h]) for h in hs]
    p8 = [_mm(p4[h], p4[h]) for h in hs]
    t = [_mm(eye - ad[h], eye + p2[h]) for h in hs]
    t = [_mm(t[h], eye + p4[h]) for h in hs]
    t = [_mm(t[h], eye + p8[h]) for h in hs]
    x = [_mm(t[h], jnp.where(off16, a[h], zero)) for h in hs]
    t = [t[h] - _mm(x[h], t[h]) for h in hs]
    x = [_mm(t[h], jnp.where(off32, a[h], zero)) for h in hs]
    t = [t[h] - _mm(x[h], t[h]) for h in hs]

    uw = [_mm(t[h], jnp.concatenate([v[h] * beta[h], kb[h] * egc[h]], axis=1)) for h in hs]
    qk = [jnp.where(incl, _mm_nt(q[h], k[h]) * decay[h], zero) for h in hs]
    qd = [q[h] * egc[h] for h in hs]
    kd = [k[h] * jnp.exp(gtot[h] - gc[h]) for h in hs]
    gts = [jnp.exp(gtot[h]) for h in hs]

    s = [s_scr[h] for h in hs]
    vns = [[] for _ in hs]
    outs = [[] for _ in hs]
    for i in range(n_chunks):
        rs = slice(i * c, (i + 1) * c)
        vn = [uw[h][rs, :HEAD_W] - _mm(uw[h][rs, HEAD_W:], s[h]) for h in hs]
        for h in hs:
            vns[h].append(vn[h])
        pad = [jnp.zeros(((n_chunks - 1 - i) * c, HEAD_W), F32)] * (i < n_chunks - 1)
        vn_all = [jnp.concatenate(vns[h] + pad, axis=0) for h in hs]
        o_i = [_mm(qd[h][rs], s[h]) + _mm(qk[h][rs], vn_all[h]) for h in hs]
        for h in hs:
            outs[h].append(o_i[h])
        s = [s[h] * gts[h][i * c:i * c + 1, :] + _mm_tn(kd[h][rs], vn[h]) for h in hs]
    for h in hs:
        s_scr[h] = s[h]

    for h in hs:
        o = jnp.concatenate(outs[h], axis=0)
        o = o * lax.rsqrt(jnp.mean(o * o, axis=-1, keepdims=True) + NORM_EPS) * nw_ref[...]
        gate = gate_ref[:, h * HEAD_W:(h + 1) * HEAD_W].astype(F32)
        o_ref[:, h * HEAD_W:(h + 1) * HEAD_W] = (o * _silu(gate)).astype(BF16)


def _dn_chunks(qn, kn, vc, beta_b, g_b, proj, norm_w):
    lt = qn.shape[1]
    hg = DN_HEADS_PER_STEP
    first_blk = (TILE - DN_ROWS) // DN_ROWS
    x_blk = TILE // DN_ROWS - first_blk
    hspec = pl.BlockSpec((hg, DN_ROWS, HEAD_W), lambda g, i: (g, i + first_blk, 0))
    wide = hg * HEAD_W
    return pl.pallas_call(
        _dn_chunk_kernel,
        grid=(N_HEADS // hg, lt // DN_ROWS - first_blk),
        in_specs=[
            hspec, hspec, hspec, hspec, hspec,
            pl.BlockSpec((DN_ROWS, wide), lambda g, i: (i + first_blk, CB_DG * HEAD_W // wide + g)),
            pl.BlockSpec((1, HEAD_W), lambda g, i: (0, 0)),
        ],
        out_specs=pl.BlockSpec((DN_ROWS, wide), lambda g, i: (jnp.maximum(i - x_blk, 0), g)),
        out_shape=jax.ShapeDtypeStruct((lt - TILE, N_HEADS * HEAD_W), BF16),
        scratch_shapes=[pltpu.VMEM((hg, HEAD_W, HEAD_W), F32)],
        compiler_params=_cparams(("parallel", "arbitrary")),
        name="dn_chunks",
    )(qn, kn, vc, beta_b, g_b, proj, norm_w)


def _mix_kernel(ya_ref, yd_ref, ga_ref, gd_ref, h_ref, wa_ref, wd_ref, wo_ref, nw_ref, wr_ref, br_ref,
                h1_ref, u2_ref, lg_ref):
    n_sub = MIX_SUB
    rs = [slice(i * (TILE // n_sub), (i + 1) * (TILE // n_sub)) for i in range(n_sub)]
    pa = [jnp.dot(ya_ref[r, :], wa_ref[...], preferred_element_type=F32) for r in rs]
    pd = [jnp.dot(yd_ref[r, :], wd_ref[...], preferred_element_type=F32) for r in rs]
    mixed = [_sigmoid(ga_ref[r, :].astype(F32)) * pa[i] + _sigmoid(gd_ref[r, :].astype(F32)) * pd[i]
             for i, r in enumerate(rs)]
    h1 = [h_ref[r, :] + jnp.dot(mixed[i].astype(BF16), wo_ref[...], preferred_element_type=F32)
          for i, r in enumerate(rs)]
    u2 = [h * lax.rsqrt(jnp.mean(h * h, axis=-1, keepdims=True) + NORM_EPS) * nw_ref[...] for h in h1]
    for i, r in enumerate(rs):
        h1_ref[r, :] = h1[i]
        u2_ref[r, :] = u2[i].astype(BF16)
        lg_ref[r, :] = _mm_f32(u2[i], wr_ref[...]) + br_ref[...]


def _mix(y_a, y_d, proj, x2d, w_a, w_d, w_o, norm_w, w_r, b_r):
    seq = y_a.shape[0]
    skip = (proj.shape[0] - seq) // TILE
    row = lambda: pl.BlockSpec((TILE, D_MODEL), lambda m: (m, 0))
    prow = lambda cb: pl.BlockSpec((TILE, D_MODEL), lambda m: (m + skip, cb))
    full = lambda: pl.BlockSpec((D_MODEL, D_MODEL), lambda m: (0, 0))
    return pl.pallas_call(
        _mix_kernel,
        grid=(seq // TILE,),
        in_specs=[
            row(), row(), prow(CB_GA // 8), prow(CB_GD // 8), row(), full(), full(), full(),
            pl.BlockSpec((1, D_MODEL), lambda m: (0, 0)),
            pl.BlockSpec((D_MODEL, LANES), lambda m: (0, 0)),
            pl.BlockSpec((1, LANES), lambda m: (0, 0)),
        ],
        out_specs=[row(), row(), pl.BlockSpec((TILE, LANES), lambda m: (m, 0))],
        out_shape=[
            jax.ShapeDtypeStruct((seq, D_MODEL), F32),
            jax.ShapeDtypeStruct((seq, D_MODEL), BF16),
            jax.ShapeDtypeStruct((seq, LANES), F32),
        ],
        compiler_params=_cparams(("parallel",)),
        name="branch_mix",
    )(y_a, y_d, proj, proj, x2d, w_a, w_d, w_o, norm_w, w_r, b_r)


def _route(logits):
    shape = logits.shape
    lane = lax.broadcasted_iota(jnp.int32, shape, 1)
    big = jnp.full(shape, 4 * LANES, jnp.int32)
    neg = jnp.full(shape, NEG, F32)
    zero = jnp.zeros(shape, F32)
    is_g = lane < N_GROUPS
    gl = jnp.where(is_g, logits, neg)
    gmax = jnp.max(gl, axis=-1, keepdims=True)
    gsum = jnp.sum(jnp.where(is_g, jnp.exp(gl - gmax), zero), axis=-1, keepdims=True)
    g_w = 1.0 / gsum
    g_i = jnp.min(jnp.where(is_g & (gl == gmax), lane, big), axis=-1, keepdims=True)
    e_lane = lane - N_GROUPS
    in_grp = (e_lane >= 0) & (e_lane < N_EXPERTS) & ((e_lane // EXPERTS_PER_GROUP) == g_i)
    el = jnp.where(in_grp, logits, neg)
    emax = jnp.max(el, axis=-1, keepdims=True)
    eexp = jnp.where(in_grp, jnp.exp(el - emax), zero)
    ep = eexp / jnp.sum(eexp, axis=-1, keepdims=True)
    minus = jnp.full(shape, -1.0, F32)
    ep1 = jnp.where(in_grp, ep, minus)
    p1 = jnp.max(ep1, axis=-1, keepdims=True)
    i1 = jnp.min(jnp.where(ep1 == p1, lane, big), axis=-1, keepdims=True)
    ep2 = jnp.where(lane == i1, minus, ep1)
    p2 = jnp.max(ep2, axis=-1, keepdims=True)
    i2 = jnp.min(jnp.where(ep2 == p2, lane, big), axis=-1, keepdims=True)
    den = p1 + p2
    cmb = g_w * (jnp.where(lane == i1, p1 / den, zero) + jnp.where(lane == i2, p2 / den, zero))
    return cmb, jnp.where(lane == g_i, 1.0, 0.0)


def _split_bf16(x):
    hi = x.astype(BF16)
    return hi, (x - hi.astype(F32)).astype(BF16)


def _moe_kernel(u_ref, lg_ref, h1_ref, tri_ref, wg_ref, wu_ref, wd_ref, fw_ref, o_ref,
                xs_scr, ys_scr, c8_scr, pos_scr, meta_ref):
    g = pl.program_id(1)
    tm = MOE_TM

    @pl.when(g == 0)
    def _():
        cmb, gsel = _route(lg_ref[...])
        lane1 = lax.broadcasted_iota(jnp.int32, (1, LANES), 1)
        rank = jnp.dot(tri_ref[...], gsel.astype(BF16), preferred_element_type=F32)
        cnt = jnp.sum(gsel, axis=0, keepdims=True)
        nch = jnp.floor((cnt + (MOE_CHUNK - 1.0)) * (1.0 / MOE_CHUNK))
        n0, n1, n2 = nch[:, 0:1], nch[:, 1:2], nch[:, 2:3]
        base = jnp.where(lane1 == 1, n0, jnp.where(lane1 == 2, n0 + n1, jnp.where(lane1 == 3, n0 + n1 + n2, 0.0)))
        for gi in range(N_GROUPS):
            meta_ref[gi] = jnp.sum(jnp.where(lane1 == gi, nch, 0.0)).astype(jnp.int32)
            meta_ref[N_GROUPS + gi] = jnp.sum(jnp.where(lane1 == gi, base, 0.0)).astype(jnp.int32)
        pos = jnp.sum(gsel * (rank + base * MOE_CHUNK), axis=-1, keepdims=True)
        pos_b = jnp.broadcast_to(pos, (tm, LANES))
        pos_scr[...] = pos_b
        pos_row = pos_b.T[0:1, :]
        rowi = lax.broadcasted_iota(jnp.int32, (MOE_CAP, tm), 0).astype(F32)
        perm = jnp.where(rowi == pos_row, 1.0, 0.0).astype(BF16)
        xs_scr[...] = jnp.dot(perm, u_ref[...], preferred_element_type=F32).astype(BF16)
        lrow = lax.broadcasted_iota(jnp.int32, (LANES, LANES), 0) - N_GROUPS
        lcol = lax.broadcasted_iota(jnp.int32, (LANES, LANES), 1)
        fold = jnp.where((lrow >= 0) & (lrow < N_EXPERTS) & ((lrow % EXPERTS_PER_GROUP) == lcol), 1.0, 0.0).astype(BF16)
        c_hi, c_lo = _split_bf16(cmb)
        c8 = jnp.dot(c_hi, fold, preferred_element_type=F32) + jnp.dot(c_lo, fold, preferred_element_type=F32)
        c8_hi, c8_lo = _split_bf16(c8)
        c8_scr[...] = (jnp.dot(perm, c8_hi, preferred_element_type=F32)
                       + jnp.dot(perm, c8_lo, preferred_element_type=F32))
        ys_scr[...] = jnp.zeros(ys_scr.shape, BF16)

    base_g = meta_ref[N_GROUPS + g]

    def chunk(jc, carry):
        r0 = pl.multiple_of((base_g + jc) * MOE_CHUNK, MOE_CHUNK)
        xc = xs_scr[pl.ds(r0, MOE_CHUNK), :]
        cw = c8_scr[pl.ds(r0, MOE_CHUNK), :]
        parts = []
        for e in range(EXPERTS_PER_GROUP):
            a = jnp.dot(xc, wg_ref[e], preferred_element_type=F32)
            b = jnp.dot(xc, wu_ref[e], preferred_element_type=F32)
            parts.append((_silu(a) * b * cw[:, e:e + 1]).astype(BF16))
        hh = jnp.concatenate(parts, axis=1)
        wd = wd_ref[...].reshape(EXPERTS_PER_GROUP * D_EXPERT, D_MODEL)
        ys_scr[pl.ds(r0, MOE_CHUNK), :] = jnp.dot(hh, wd, preferred_element_type=F32).astype(BF16)
        return carry

    lax.fori_loop(0, meta_ref[g], chunk, 0)

    @pl.when(g == N_GROUPS - 1)
    def _():
        pos_w = jnp.concatenate([pos_scr[...]] * (MOE_CAP // LANES), axis=1)
        lane = lax.broadcasted_iota(jnp.int32, (tm, MOE_CAP), 1).astype(F32)
        unperm = jnp.where(lane == pos_w, 1.0, 0.0).astype(BF16)
        hf = h1_ref[...] + jnp.dot(unperm, ys_scr[...], preferred_element_type=F32)
        o_ref[...] = hf * lax.rsqrt(jnp.mean(hf * hf, axis=-1, keepdims=True) + NORM_EPS) * fw_ref[...]


def _moe(u2, logits, h1, tri, w_gate, w_up, w_down, final_w):
    seq = u2.shape[0]
    once = pl.Buffered(1)
    row = lambda w, **kw: pl.BlockSpec((MOE_TM, w), lambda m, g: (m, 0), **kw)
    return pl.pallas_call(
        _moe_kernel,
        grid=(seq // MOE_TM, N_GROUPS),
        in_specs=[
            row(D_MODEL), row(LANES), row(D_MODEL, pipeline_mode=once),
            pl.BlockSpec((MOE_TM, MOE_TM), lambda m, g: (0, 0), pipeline_mode=once),
            pl.BlockSpec((EXPERTS_PER_GROUP, D_MODEL, D_EXPERT), lambda m, g: (g, 0, 0)),
            pl.BlockSpec((EXPERTS_PER_GROUP, D_MODEL, D_EXPERT), lambda m, g: (g, 0, 0)),
            pl.BlockSpec((EXPERTS_PER_GROUP, D_EXPERT, D_MODEL), lambda m, g: (g, 0, 0)),
            pl.BlockSpec((1, D_MODEL), lambda m, g: (0, 0)),
        ],
        out_specs=row(D_MODEL, pipeline_mode=once),
        out_shape=jax.ShapeDtypeStruct((seq, D_MODEL), F32),
        scratch_shapes=[pltpu.VMEM((MOE_CAP, D_MODEL), BF16), pltpu.VMEM((MOE_CAP, D_MODEL), BF16),
                        pltpu.VMEM((MOE_CAP, LANES), F32), pltpu.VMEM((MOE_TM, LANES), F32),
                        pltpu.SMEM((2 * N_GROUPS,), jnp.int32)],
        compiler_params=pltpu.CompilerParams(dimension_semantics=("parallel", "arbitrary"),
                                             vmem_limit_bytes=MOE_VMEM_LIMIT),
        name="moe_final",
    )(u2, logits, h1, tri, w_gate, w_up, w_down, final_w)


def kernel(x, meta_tokens, norm1_w, w_in, lambda_q1, lambda_k1, lambda_q2, lambda_k2, da_subln_w, dn_conv_w,
           dn_A_log, dn_dt_bias, dn_norm_w, w_branch_attn, w_branch_dn, w_out, norm2_w, router_group_w,
           router_group_b, router_expert_w, router_expert_b, expert_w_gate, expert_w_up, expert_w_down,
           final_norm_w):
    assert x.shape[0] == 1 and x.shape[2] == D_MODEL and x.shape[1] % ATT_TQ == 0
    assert w_in.shape[0] == 1, "single layer: the block output at the meta rows is unused"
    seq = x.shape[1]
    lt = TILE + seq
    assert lt <= 256 * LANES, "key positions must split into two exact bf16 terms"
    first = TILE - N_META

    x2d = x[0]
    meta_tile = jnp.concatenate([jnp.zeros((first, D_MODEL), F32), meta_tokens.astype(F32)], axis=0)

    w = w_in[0]
    d = D_MODEL
    w_big = jnp.concatenate([w[:, 3 * d:6 * d], w[:, d:2 * d], w[:, 6 * d:7 * d], w[:, 7 * d + 2 * N_HEADS:]],
                            axis=1).astype(BF16)
    w_qv_t = jnp.concatenate([w[:, :d] * DA_HEAD_DIM ** -0.5, w[:, 2 * d:3 * d]], axis=1).T.astype(BF16)
    w_small = jnp.pad(w[:, 7 * d:7 * d + 2 * N_HEADS], ((0, 0), (0, LANES - 2 * N_HEADS)))
    proj, qvt, small = _inproj(x2d, meta_tile, norm1_w, w_big, w_qv_t, w_small)

    pos = jnp.arange(lt, dtype=jnp.int32)
    kpos = jnp.stack([(pos >> 7) << 7, pos & 127, jnp.ones_like(pos)], axis=1).astype(BF16)
    kpos = jnp.pad(kpos, ((0, 0), (0, LANES - 3)))
    slopes = jnp.broadcast_to(jnp.asarray(ALIBI_SLOPES, F32)[:, None], (N_HEADS, LANES))
    lam_tab = jnp.pad(jnp.concatenate([lambda_q1, lambda_k1, lambda_q2, lambda_k2], axis=0),
                      ((0, 4), (0, LANES - DA_HEAD_DIM)))
    y_a = _attention(qvt, proj, kpos, slopes, lam_tab, da_subln_w, seq, first)

    pad16 = lambda t, off: jnp.pad(t, ((0, 0), (off, LANES - N_HEADS - off)))
    qn, kn, vc, beta_b, g_b = _dn_prep(proj, small, dn_conv_w[0], pad16(dn_A_log, N_HEADS),
                                       pad16(dn_dt_bias, N_HEADS), first, lt)
    y_d = _dn_chunks(qn, kn, vc, beta_b, g_b, proj, dn_norm_w)

    w_r = jnp.pad(jnp.concatenate([router_group_w[0], router_expert_w[0]], axis=1),
                  ((0, 0), (0, LANES - N_GROUPS - N_EXPERTS)))
    b_r = jnp.pad(jnp.concatenate([router_group_b, router_expert_b], axis=1),
                  ((0, 0), (0, LANES - N_GROUPS - N_EXPERTS)))
    h1, u2, logits = _mix(y_a, y_d, proj, x2d, w_branch_attn[0].astype(BF16), w_branch_dn[0].astype(BF16),
                          w_out[0].astype(BF16), norm2_w, w_r, b_r)

    tri = jnp.tril(jnp.ones((MOE_TM, MOE_TM), BF16), -1)
    out = _moe(u2, logits, h1, tri, expert_w_gate[0].astype(BF16), expert_w_up[0].astype(BF16),
               expert_w_down[0].astype(BF16), final_norm_w[None, :])
    return out[None]
```

```python
import functools
import math

import jax
import jax.numpy as jnp
from jax import lax
from jax.experimental import pallas as pl
from jax.experimental.pallas import tpu as pltpu

F32 = jnp.float32
BF16 = jnp.bfloat16
HIGHEST = lax.Precision.HIGHEST

D_MODEL = 1024
N_META = 16
NORM_EPS = 1e-6
N_HEADS = 8
HEAD_W = 128
DA_HEAD_DIM = 64
DN_CONV = 4
DN_CHUNK = 64
N_GROUPS = 4
EXPERTS_PER_GROUP = 8
N_EXPERTS = 32
D_EXPERT = 256
LAMBDA_INIT = 0.8 - 0.6 * math.exp(-0.3 * 0)
ALIBI_SLOPES = tuple(2.0 ** (-8.0 * (h + 1) / N_HEADS) for h in range(N_HEADS))

TILE = 512
ATT_TQ = 2048
Q_CHAIN = 256
V_ROWS = 144
DN_ROWS = 256
DN_HEADS_PER_STEP = 8
MIX_SUB = 2
LANES = 128
NEG = -1e30
VMEM_LIMIT = 56 * 1024 * 1024
MOE_TM = 1024
MOE_CHUNK = 128
MOE_CAP = MOE_TM + N_GROUPS * MOE_CHUNK
MOE_VMEM_LIMIT = 60 * 1024 * 1024

CB_AK, CB_DG, CB_GA, CB_GD = 0, 8, 16, 24
PROJ_COLS = 32 * 128


def _cparams(sem):
    return pltpu.CompilerParams(dimension_semantics=sem, vmem_limit_bytes=VMEM_LIMIT)


def _mm(a, b):
    return jnp.dot(a.astype(BF16), b.astype(BF16), preferred_element_type=F32)


def _mm_nt(a, b):
    return lax.dot_general(a.astype(BF16), b.astype(BF16), (((1,), (1,)), ((), ())),
                           preferred_element_type=F32)


def _mm_tn(a, b):
    return lax.dot_general(a.astype(BF16), b.astype(BF16), (((0,), (0,)), ((), ())),
                           preferred_element_type=F32)


def _mm_f32(a, b):
    return jnp.dot(a, b, precision=HIGHEST, preferred_element_type=F32)


def _silu(x):
    return x * (1.0 / (1.0 + jnp.exp(-x)))


def _sigmoid(x):
    return 1.0 / (1.0 + jnp.exp(-x))


def _inproj_kernel(x_ref, meta_ref, nw_ref, wa_ref, wb_ref, wt_ref, ws_ref, cw_ref, al_ref, dtb_ref,
                   o_ref, ot_ref, q_ref, k_ref, v_ref, bg_ref, xe_scr, *, first_row, last_row):
    m = pl.program_id(0)
    tm = TILE

    @pl.when(m == 0)
    def _():
        xe_scr[0:8, :] = jnp.zeros((8, xe_scr.shape[1]), F32)

    x = jnp.where(m == 0, meta_ref[...], x_ref[...])
    u = x * lax.rsqrt(jnp.mean(x * x, axis=-1, keepdims=True) + NORM_EPS) * nw_ref[...]
    ub = u.astype(BF16)
    xe_scr[8:, :] = jnp.dot(ub, wa_ref[...], preferred_element_type=F32)
    o_ref[...] = jnp.dot(ub, wb_ref[...], preferred_element_type=F32).astype(BF16)
    ot_ref[0] = lax.dot_general(wt_ref[...], ub, (((1,), (1,)), ((), ())),
                                preferred_element_type=F32).astype(BF16)
    sm = _mm_f32(u, ws_ref[...])

    cw = cw_ref[...]
    acc = xe_scr[8:, :] * cw[DN_CONV - 1:DN_CONV, :]
    for j in range(DN_CONV - 1):
        acc = acc + xe_scr[pl.ds(8 - (DN_CONV - 1) + j, tm), :] * cw[j:j + 1, :]
    xe_scr[0:8, :] = xe_scr[tm:tm + 8, :]
    y = _silu(acc)

    width = N_HEADS * HEAD_W
    for h in range(N_HEADS):
        qh = y[:, h * HEAD_W:(h + 1) * HEAD_W]
        kh = y[:, width + h * HEAD_W: width + (h + 1) * HEAD_W]
        vh = y[:, 2 * width + h * HEAD_W: 2 * width + (h + 1) * HEAD_W]
        qh = qh * lax.rsqrt(jnp.sum(qh * qh, axis=-1, keepdims=True) + 1e-6) * (HEAD_W ** -0.5)
        kh = kh * lax.rsqrt(jnp.sum(kh * kh, axis=-1, keepdims=True) + 1e-6)
        q_ref[h] = qh.astype(BF16)
        k_ref[h] = kh.astype(BF16)
        v_ref[h] = vh.astype(BF16)

    rowid = m * tm + lax.broadcasted_iota(jnp.int32, (tm, LANES), 0)
    lane = lax.broadcasted_iota(jnp.int32, (tm, LANES), 1)
    real = (rowid >= first_row) & (rowid < last_row)
    a = sm + dtb_ref[...]
    softplus = jnp.maximum(a, 0.0) + jnp.log(1.0 + jnp.exp(-jnp.abs(a)))
    g_all = -jnp.exp(al_ref[...]) * softplus
    bg_ref[...] = jnp.where(real, jnp.where(lane < N_HEADS, _sigmoid(sm), g_all), 0.0)


def _inproj(x2d, meta_tile, norm_w, w_a, w_b, w_qv_t, w_small, conv_w, a_log_row, dtb_row, first_row, last_row):
    lt = TILE + x2d.shape[0]
    n_t = lt // TILE
    c = 3 * N_HEADS * HEAD_W
    once = pl.Buffered(1)
    const = lambda shape: pl.BlockSpec(shape, lambda m: (0,) * len(shape), pipeline_mode=once)
    kern = functools.partial(_inproj_kernel, first_row=first_row, last_row=last_row)
    hshape = jax.ShapeDtypeStruct((N_HEADS, lt, HEAD_W), BF16)
    hspec = pl.BlockSpec((N_HEADS, TILE, HEAD_W), lambda m: (0, m, 0))
    return pl.pallas_call(
        kern,
        grid=(n_t,),
        in_specs=[
            pl.BlockSpec((TILE, D_MODEL), lambda m: (jnp.maximum(m - 1, 0), 0)),
            const((TILE, D_MODEL)), const((1, D_MODEL)), const((D_MODEL, c)), const((D_MODEL, PROJ_COLS)),
            const((2 * D_MODEL, D_MODEL)), const((D_MODEL, LANES)), const((DN_CONV, c)),
            const((1, LANES)), const((1, LANES)),
        ],
        out_specs=[
            pl.BlockSpec((TILE, PROJ_COLS), lambda m: (m, 0)),
            pl.BlockSpec((1, 2 * D_MODEL, TILE), lambda m: (m, 0, 0)),
            hspec, hspec, hspec,
            pl.BlockSpec((TILE, LANES), lambda m: (m, 0)),
        ],
        out_shape=[
            jax.ShapeDtypeStruct((lt, PROJ_COLS), BF16),
            jax.ShapeDtypeStruct((n_t, 2 * D_MODEL, TILE), BF16),
            hshape, hshape, hshape,
            jax.ShapeDtypeStruct((lt, LANES), F32),
        ],
        scratch_shapes=[pltpu.VMEM((TILE + 8, c), F32)],
        compiler_params=pltpu.CompilerParams(dimension_semantics=("arbitrary",), vmem_limit_bytes=MOE_VMEM_LIMIT),
        name="inproj",
    )(x2d, meta_tile, norm_w, w_a, w_b, w_qv_t, w_small, conv_w, a_log_row, dtb_row)


def _attn_kernel(*refs, tq, first_key):
    tpq = tq // TILE
    q_refs = refs[:tpq]
    k_ref, kpos_ref, vt_ref, slope_ref, lam_ref, sw_ref, o_ref, m_ref, a_ref, s_ref = refs[tpq:]
    h = pl.program_id(0)
    j = pl.program_id(1)
    tk = TILE
    n_q = tq // Q_CHAIN

    qt = jnp.concatenate([r[0] for r in q_refs], axis=1)
    slope = slope_ref[pl.ds(h, 1), :][:, :1]
    qbase = TILE + j * tq
    rowi = lax.broadcasted_iota(jnp.int32, (HEAD_W, tq), 0)
    brow = jnp.where(rowi < 2, slope, jnp.where(rowi == 2, -(slope * qbase.astype(F32)), 0.0)).astype(BF16)
    z64 = jnp.zeros((DA_HEAD_DIM, tq), BF16)
    w1 = jnp.concatenate([qt[:DA_HEAD_DIM], z64, brow], axis=0)
    w2 = jnp.concatenate([z64, qt[DA_HEAD_DIM:], brow], axis=0)
    chains = [(w, hf) for w in (w1, w2) for hf in range(n_q)]
    ones_rows = jnp.where(lax.broadcasted_iota(jnp.int32, (V_ROWS - HEAD_W, tk), 0) == 0, 1.0, 0.0).astype(BF16)

    m_ref[...] = jnp.full(m_ref.shape, NEG, F32)
    a_ref[...] = jnp.zeros(a_ref.shape, F32)

    def stage(ki, modes, lo=0, rows=tk):
        start = pl.multiple_of(ki * tk, tk) + lo
        kt = jnp.concatenate([k_ref[pl.ds(start, rows), :], kpos_ref[pl.ds(start, rows), :]], axis=1)
        vt = jnp.concatenate([vt_ref[ki][:, lo:lo + rows], ones_rows[:, :rows]], axis=0)
        key = start + lax.broadcasted_iota(jnp.int32, (rows, Q_CHAIN), 0)
        lane = lax.broadcasted_iota(jnp.int32, (rows, Q_CHAIN), 1)
        tile_max = {}

        def score(c):
            w, hf = chains[c]
            st = jnp.dot(kt, w[:, hf * Q_CHAIN:(hf + 1) * Q_CHAIN], preferred_element_type=F32)
            if modes[hf] == "front":
                st = jnp.where(key >= first_key, st, NEG)
            elif modes[hf] == "causal":
                st = jnp.where(key <= lane + (qbase + hf * Q_CHAIN), st, NEG)
            s_ref[c, :rows] = st
            tile_max[c] = jnp.max(st, axis=0, keepdims=True)

        def accumulate(c):
            m_old = m_ref[c]
            m_new = jnp.maximum(m_old, tile_max[c])
            alpha = jnp.exp(m_old - m_new)
            p = jnp.exp(s_ref[c, :rows] - m_new).astype(BF16)
            a_ref[c] = alpha * a_ref[c] + jnp.dot(vt, p, preferred_element_type=F32)
            m_ref[c] = m_new

        live = [c for c, (_, hf) in enumerate(chains) if modes[hf] != "skip"]
        return live, score, accumulate

    def tile(ki, modes, **kw):
        live, score, accumulate = stage(ki, modes, **kw)
        for c in live:
            score(c)
        for c in live:
            accumulate(c)

    tile(0, ["front"] * n_q, lo=tk - LANES, rows=LANES)

    def body(ki, carry):
        tile(ki, ["none"] * n_q)
        return carry

    lax.fori_loop(1, tpq * j + 1, body, 0)
    for d in range(tpq):
        modes = ["skip" if (hf + 1) * Q_CHAIN <= d * tk else "none" if hf * Q_CHAIN >= (d + 1) * tk else "causal"
                 for hf in range(n_q)]
        tile(tpq * j + 1 + d, modes)

    lt = lam_ref[...]
    lam = (jnp.exp(jnp.sum(lt[0:1] * lt[1:2], axis=-1, keepdims=True))
           - jnp.exp(jnp.sum(lt[2:3] * lt[3:4], axis=-1, keepdims=True)) + LAMBDA_INIT)
    for hf in range(n_q):
        a1 = a_ref[hf]
        a2 = a_ref[n_q + hf]
        ot = a1[:HEAD_W] / a1[HEAD_W:HEAD_W + 1] - lam * (a2[:HEAD_W] / a2[HEAD_W:HEAD_W + 1])
        o = ot.T
        o = o * lax.rsqrt(jnp.mean(o * o, axis=-1, keepdims=True) + NORM_EPS) * sw_ref[...]
        o_ref[hf * Q_CHAIN:(hf + 1) * Q_CHAIN, :] = (o * (1.0 - LAMBDA_INIT)).astype(BF16)


def _attention(qvt, proj, kpos, slopes, lam_tab, subln_w, seq, first_key):
    n_t = qvt.shape[0]
    lt = n_t * TILE
    tq = ATT_TQ
    tpq = tq // TILE
    assert tq % TILE == 0 and seq % tq == 0 and first_key >= TILE - LANES
    kern = functools.partial(_attn_kernel, tq=tq, first_key=first_key)
    n_chain = 2 * tq // Q_CHAIN
    scratch = [pltpu.VMEM((n_chain, 1, Q_CHAIN), F32), pltpu.VMEM((n_chain, V_ROWS, Q_CHAIN), F32),
               pltpu.VMEM((n_chain, TILE, Q_CHAIN), F32)]
    q_spec = lambda d: pl.BlockSpec((1, HEAD_W, TILE), lambda h, j: (1 + d + tpq * j, h, 0))
    once = pl.Buffered(1)
    return pl.pallas_call(
        kern,
        grid=(N_HEADS, seq // tq),
        in_specs=[q_spec(d) for d in range(tpq)] + [
            pl.BlockSpec((lt, HEAD_W), lambda h, j: (0, CB_AK + h), pipeline_mode=once),
            pl.BlockSpec((lt, LANES), lambda h, j: (0, 0), pipeline_mode=once),
            pl.BlockSpec((n_t, HEAD_W, TILE), lambda h, j: (0, N_HEADS + h, 0), pipeline_mode=once),
            pl.BlockSpec((8, LANES), lambda h, j: (0, 0)),
            pl.BlockSpec((8, LANES), lambda h, j: (0, 0)),
            pl.BlockSpec((1, HEAD_W), lambda h, j: (0, 0)),
        ],
        out_specs=pl.BlockSpec((tq, HEAD_W), lambda h, j: (j, h)),
        out_shape=jax.ShapeDtypeStruct((seq, N_HEADS * HEAD_W), BF16),
        scratch_shapes=scratch,
        compiler_params=_cparams(("parallel", "arbitrary")),
        name="diff_attn",
    )(*([qvt] * tpq), proj, kpos, qvt, slopes, lam_tab, subln_w)


def _dn_chunk_kernel(q_ref, k_ref, v_ref, bg_ref, gate_ref, nw_ref, o_ref, s_scr):
    @pl.when(pl.program_id(1) == 0)
    def _():
        s_scr[...] = jnp.zeros(s_scr.shape, F32)

    r = DN_ROWS
    c = DN_CHUNK
    n_chunks = r // c
    hs = range(q_ref.shape[0])
    h0 = pl.program_id(0) * q_ref.shape[0]

    row = lax.broadcasted_iota(jnp.int32, (r, r), 0)
    col = lax.broadcasted_iota(jnp.int32, (r, r), 1)
    same64 = (row // c) == (col // c)
    same32 = (row // 32) == (col // 32)
    same16 = (row // 16) == (col // 16)
    incl = same64 & (row >= col)
    strict = same64 & (row > col)
    off16 = same32 & jnp.logical_not(same16)
    off32 = same64 & jnp.logical_not(same32)
    zero = jnp.zeros((r, r), F32)
    incl_b = jnp.where(incl, 1.0, 0.0).astype(BF16)
    eye = jnp.where(row == col, 1.0, 0.0)

    q = [q_ref[h].astype(F32) for h in hs]
    k = [k_ref[h].astype(F32) for h in hs]
    v = [v_ref[h].astype(F32) for h in hs]
    bg = bg_ref[...]
    lane = lax.broadcasted_iota(jnp.int32, bg.shape, 1)
    pick = lambda l: jnp.broadcast_to(jnp.sum(jnp.where(lane == l, bg, 0.0), axis=-1, keepdims=True), bg.shape)
    beta = [pick(h0 + h) for h in hs]
    g_in = [pick(N_HEADS + h0 + h) for h in hs]

    def cumsum(g):
        g_hi = g.astype(BF16)
        r1 = g - g_hi.astype(F32)
        g_mid = r1.astype(BF16)
        g_lo = (r1 - g_mid.astype(F32)).astype(BF16)
        parts = jnp.dot(incl_b, jnp.concatenate([g_hi, g_mid, g_lo], axis=1), preferred_element_type=F32)
        return (parts[:, :LANES] + parts[:, LANES:2 * LANES]) + parts[:, 2 * LANES:]

    gc = [cumsum(g_in[h]) for h in hs]
    gc_row = [gc[h].T[0:1, :] for h in hs]
    diff = [jnp.concatenate([gc[h], gc[h]], axis=1) - gc_row[h] for h in hs]
    decay = [jnp.where(incl, jnp.exp(jnp.where(incl, diff[h], zero)), zero) for h in hs]
    gtot = [jnp.concatenate([jnp.broadcast_to(gc[h][(i + 1) * c - 1:(i + 1) * c, :], (c, LANES))
                             for i in range(n_chunks)], axis=0) for h in hs]
    egc = [jnp.exp(gc[h]) for h in hs]
    kb = [k[h] * beta[h] for h in hs]
    a = [jnp.where(strict, _mm_nt(kb[h], k[h]) * decay[h], zero) for h in hs]

    ad = [jnp.where(same16, a[h], zero) for h in hs]
    p2 = [_mm(ad[h], ad[h]) for h in hs]
    p4 = [_mm(p2[h], p2[h]) for h in hs]
    p8 = [_mm(p4[h], p4[h]) for h in hs]
    t = [_mm(eye - ad[h], eye + p2[h]) for h in hs]
    t = [_mm(t[h], eye + p4[h]) for h in hs]
    t = [_mm(t[h], eye + p8[h]) for h in hs]
    x = [_mm(t[h], jnp.where(off16, a[h], zero)) for h in hs]
    t = [t[h] - _mm(x[h], t[h]) for h in hs]
    x = [_mm(t[h], jnp.where(off32, a[h], zero)) for h in hs]
    t = [t[h] - _mm(x[h], t[h]) for h in hs]

    uw = [_mm(t[h], jnp.concatenate([v[h] * beta[h], kb[h] * egc[h]], axis=1)) for h in hs]
    qk = [jnp.where(incl, _mm_nt(q[h], k[h]) * decay[h], zero) for h in hs]
    qd = [q[h] * egc[h] for h in hs]
    kd = [k[h] * jnp.exp(gtot[h] - gc[h]) for h in hs]
    gts = [jnp.exp(gtot[h]) for h in hs]

    s = [s_scr[h] for h in hs]
    vns = [[] for _ in hs]
    outs = [[] for _ in hs]
    for i in range(n_chunks):
        rs = slice(i * c, (i + 1) * c)
        vn = [uw[h][rs, :HEAD_W] - _mm(uw[h][rs, HEAD_W:], s[h]) for h in hs]
        for h in hs:
            vns[h].append(vn[h])
        pad = [jnp.zeros(((n_chunks - 1 - i) * c, HEAD_W), F32)] * (i < n_chunks - 1)
        vn_all = [jnp.concatenate(vns[h] + pad, axis=0) for h in hs]
        o_i = [_mm(qd[h][rs], s[h]) + _mm(qk[h][rs], vn_all[h]) for h in hs]
        for h in hs:
            outs[h].append(o_i[h])
        s = [s[h] * gts[h][i * c:i * c + 1, :] + _mm_tn(kd[h][rs], vn[h]) for h in hs]
    for h in hs:
        s_scr[h] = s[h]

    for h in hs:
        o = jnp.concatenate(outs[h], axis=0)
        o = o * lax.rsqrt(jnp.mean(o * o, axis=-1, keepdims=True) + NORM_EPS) * nw_ref[...]
        gate = gate_ref[:, h * HEAD_W:(h + 1) * HEAD_W].astype(F32)
        o_ref[:, h * HEAD_W:(h + 1) * HEAD_W] = (o * _silu(gate)).astype(BF16)


def _dn_chunks(qn, kn, vc, bg, proj, norm_w):
    lt = qn.shape[1]
    hg = DN_HEADS_PER_STEP
    first_blk = (TILE - DN_ROWS) // DN_ROWS
    x_blk = TILE // DN_ROWS - first_blk
    hspec = pl.BlockSpec((hg, DN_ROWS, HEAD_W), lambda g, i: (g, i + first_blk, 0))
    wide = hg * HEAD_W
    return pl.pallas_call(
        _dn_chunk_kernel,
        grid=(N_HEADS // hg, lt // DN_ROWS - first_blk),
        in_specs=[
            hspec, hspec, hspec,
            pl.BlockSpec((DN_ROWS, LANES), lambda g, i: (i + first_blk, 0)),
            pl.BlockSpec((DN_ROWS, wide), lambda g, i: (i + first_blk, CB_DG * HEAD_W // wide + g)),
            pl.BlockSpec((1, HEAD_W), lambda g, i: (0, 0)),
        ],
        out_specs=pl.BlockSpec((DN_ROWS, wide), lambda g, i: (jnp.maximum(i - x_blk, 0), g)),
        out_shape=jax.ShapeDtypeStruct((lt - TILE, N_HEADS * HEAD_W), BF16),
        scratch_shapes=[pltpu.VMEM((hg, HEAD_W, HEAD_W), F32)],
        compiler_params=_cparams(("parallel", "arbitrary")),
        name="dn_chunks",
    )(qn, kn, vc, bg, proj, norm_w)


def _mix_kernel(ya_ref, yd_ref, ga_ref, gd_ref, h_ref, wa_ref, wd_ref, wo_ref, nw_ref, wr_ref, br_ref,
                h1_ref, u2_ref, lg_ref):
    n_sub = MIX_SUB
    rs = [slice(i * (TILE // n_sub), (i + 1) * (TILE // n_sub)) for i in range(n_sub)]
    pa = [jnp.dot(ya_ref[r, :], wa_ref[...], preferred_element_type=F32) for r in rs]
    pd = [jnp.dot(yd_ref[r, :], wd_ref[...], preferred_element_type=F32) for r in rs]
    mixed = [_sigmoid(ga_ref[r, :].astype(F32)) * pa[i] + _sigmoid(gd_ref[r, :].astype(F32)) * pd[i]
             for i, r in enumerate(rs)]
    h1 = [h_ref[r, :] + jnp.dot(mixed[i].astype(BF16), wo_ref[...], preferred_element_type=F32)
          for i, r in enumerate(rs)]
    u2 = [h * lax.rsqrt(jnp.mean(h * h, axis=-1, keepdims=True) + NORM_EPS) * nw_ref[...] for h in h1]
    for i, r in enumerate(rs):
        h1_ref[r, :] = h1[i]
        u2_ref[r, :] = u2[i].astype(BF16)
        lg_ref[r, :] = _mm_f32(u2[i], wr_ref[...]) + br_ref[...]


def _mix(y_a, y_d, proj, x2d, w_a, w_d, w_o, norm_w, w_r, b_r):
    seq = y_a.shape[0]
    skip = (proj.shape[0] - seq) // TILE
    row = lambda: pl.BlockSpec((TILE, D_MODEL), lambda m: (m, 0))
    prow = lambda cb: pl.BlockSpec((TILE, D_MODEL), lambda m: (m + skip, cb))
    full = lambda: pl.BlockSpec((D_MODEL, D_MODEL), lambda m: (0, 0))
    return pl.pallas_call(
        _mix_kernel,
        grid=(seq // TILE,),
        in_specs=[
            row(), row(), prow(CB_GA // 8), prow(CB_GD // 8), row(), full(), full(), full(),
            pl.BlockSpec((1, D_MODEL), lambda m: (0, 0)),
            pl.BlockSpec((D_MODEL, LANES), lambda m: (0, 0)),
            pl.BlockSpec((1, LANES), lambda m: (0, 0)),
        ],
        out_specs=[row(), row(), pl.BlockSpec((TILE, LANES), lambda m: (m, 0))],
        out_shape=[
            jax.ShapeDtypeStruct((seq, D_MODEL), F32),
            jax.ShapeDtypeStruct((seq, D_MODEL), BF16),
            jax.ShapeDtypeStruct((seq, LANES), F32),
        ],
        compiler_params=_cparams(("parallel",)),
        name="branch_mix",
    )(y_a, y_d, proj, proj, x2d, w_a, w_d, w_o, norm_w, w_r, b_r)


def _route(logits):
    shape = logits.shape
    lane = lax.broadcasted_iota(jnp.int32, shape, 1)
    big = jnp.full(shape, 4 * LANES, jnp.int32)
    neg = jnp.full(shape, NEG, F32)
    zero = jnp.zeros(shape, F32)
    is_g = lane < N_GROUPS
    gl = jnp.where(is_g, logits, neg)
    gmax = jnp.max(gl, axis=-1, keepdims=True)
    gsum = jnp.sum(jnp.where(is_g, jnp.exp(gl - gmax), zero), axis=-1, keepdims=True)
    g_w = 1.0 / gsum
    g_i = jnp.min(jnp.where(is_g & (gl == gmax), lane, big), axis=-1, keepdims=True)
    e_lane = lane - N_GROUPS
    in_grp = (e_lane >= 0) & (e_lane < N_EXPERTS) & ((e_lane // EXPERTS_PER_GROUP) == g_i)
    el = jnp.where(in_grp, logits, neg)
    emax = jnp.max(el, axis=-1, keepdims=True)
    eexp = jnp.where(in_grp, jnp.exp(el - emax), zero)
    ep = eexp / jnp.sum(eexp, axis=-1, keepdims=True)
    minus = jnp.full(shape, -1.0, F32)
    ep1 = jnp.where(in_grp, ep, minus)
    p1 = jnp.max(ep1, axis=-1, keepdims=True)
    i1 = jnp.min(jnp.where(ep1 == p1, lane, big), axis=-1, keepdims=True)
    ep2 = jnp.where(lane == i1, minus, ep1)
    p2 = jnp.max(ep2, axis=-1, keepdims=True)
    i2 = jnp.min(jnp.where(ep2 == p2, lane, big), axis=-1, keepdims=True)
    den = p1 + p2
    cmb = g_w * (jnp.where(lane == i1, p1 / den, zero) + jnp.where(lane == i2, p2 / den, zero))
    return cmb, jnp.where(lane == g_i, 1.0, 0.0)


def _split_bf16(x):
    hi = x.astype(BF16)
    return hi, (x - hi.astype(F32)).astype(BF16)


def _moe_kernel(u_ref, lg_ref, h1_ref, tri_ref, wg_ref, wu_ref, wd_ref, fw_ref, o_ref,
                xs_scr, ys_scr, c8_scr, pos_scr, meta_ref):
    g = pl.program_id(1)
    tm = MOE_TM

    @pl.when(g == 0)
    def _():
        cmb, gsel = _route(lg_ref[...])
        lane1 = lax.broadcasted_iota(jnp.int32, (1, LANES), 1)
        rank = jnp.dot(tri_ref[...], gsel.astype(BF16), preferred_element_type=F32)
        cnt = jnp.sum(gsel, axis=0, keepdims=True)
        nch = jnp.floor((cnt + (MOE_CHUNK - 1.0)) * (1.0 / MOE_CHUNK))
        n0, n1, n2 = nch[:, 0:1], nch[:, 1:2], nch[:, 2:3]
        base = jnp.where(lane1 == 1, n0, jnp.where(lane1 == 2, n0 + n1, jnp.where(lane1 == 3, n0 + n1 + n2, 0.0)))
        for gi in range(N_GROUPS):
            meta_ref[gi] = jnp.sum(jnp.where(lane1 == gi, nch, 0.0)).astype(jnp.int32)
            meta_ref[N_GROUPS + gi] = jnp.sum(jnp.where(lane1 == gi, base, 0.0)).astype(jnp.int32)
        pos = jnp.sum(gsel * (rank + base * MOE_CHUNK), axis=-1, keepdims=True)
        pos_b = jnp.broadcast_to(pos, (tm, LANES))
        pos_scr[...] = pos_b
        pos_row = pos_b.T[0:1, :]
        rowi = lax.broadcasted_iota(jnp.int32, (MOE_CAP, tm), 0).astype(F32)
        perm = jnp.where(rowi == pos_row, 1.0, 0.0).astype(BF16)
        xs_scr[...] = jnp.dot(perm, u_ref[...], preferred_element_type=F32).astype(BF16)
        lrow = lax.broadcasted_iota(jnp.int32, (LANES, LANES), 0) - N_GROUPS
        lcol = lax.broadcasted_iota(jnp.int32, (LANES, LANES), 1)
        fold = jnp.where((lrow >= 0) & (lrow < N_EXPERTS) & ((lrow % EXPERTS_PER_GROUP) == lcol), 1.0, 0.0).astype(BF16)
        c_hi, c_lo = _split_bf16(cmb)
        c8 = jnp.dot(c_hi, fold, preferred_element_type=F32) + jnp.dot(c_lo, fold, preferred_element_type=F32)
        c8_hi, c8_lo = _split_bf16(c8)
        c8_scr[...] = (jnp.dot(perm, c8_hi, preferred_element_type=F32)
                       + jnp.dot(perm, c8_lo, preferred_element_type=F32))
        ys_scr[...] = jnp.zeros(ys_scr.shape, BF16)

    base_g = meta_ref[N_GROUPS + g]

    def chunk(jc, carry):
        r0 = pl.multiple_of((base_g + jc) * MOE_CHUNK, MOE_CHUNK)
        xc = xs_scr[pl.ds(r0, MOE_CHUNK), :]
        cw = c8_scr[pl.ds(r0, MOE_CHUNK), :]
        parts = []
        for e in range(EXPERTS_PER_GROUP):
            a = jnp.dot(xc, wg_ref[e], preferred_element_type=F32)
            b = jnp.dot(xc, wu_ref[e], preferred_element_type=F32)
            parts.append((_silu(a) * b * cw[:, e:e + 1]).astype(BF16))
        hh = jnp.concatenate(parts, axis=1)
        wd = wd_ref[...].reshape(EXPERTS_PER_GROUP * D_EXPERT, D_MODEL)
        ys_scr[pl.ds(r0, MOE_CHUNK), :] = jnp.dot(hh, wd, preferred_element_type=F32).astype(BF16)
        return carry

    lax.fori_loop(0, meta_ref[g], chunk, 0)

    @pl.when(g == N_GROUPS - 1)
    def _():
        pos_w = jnp.concatenate([pos_scr[...]] * (MOE_CAP // LANES), axis=1)
        lane = lax.broadcasted_iota(jnp.int32, (tm, MOE_CAP), 1).astype(F32)
        unperm = jnp.where(lane == pos_w, 1.0, 0.0).astype(BF16)
        hf = h1_ref[...] + jnp.dot(unperm, ys_scr[...], preferred_element_type=F32)
        o_ref[...] = hf * lax.rsqrt(jnp.mean(hf * hf, axis=-1, keepdims=True) + NORM_EPS) * fw_ref[...]


def _moe(u2, logits, h1, tri, w_gate, w_up, w_down, final_w):
    seq = u2.shape[0]
    once = pl.Buffered(1)
    row = lambda w, **kw: pl.BlockSpec((MOE_TM, w), lambda m, g: (m, 0), **kw)
    return pl.pallas_call(
        _moe_kernel,
        grid=(seq // MOE_TM, N_GROUPS),
        in_specs=[
            row(D_MODEL), row(LANES), row(D_MODEL, pipeline_mode=once),
            pl.BlockSpec((MOE_TM, MOE_TM), lambda m, g: (0, 0), pipeline_mode=once),
            pl.BlockSpec((EXPERTS_PER_GROUP, D_MODEL, D_EXPERT), lambda m, g: (g, 0, 0)),
            pl.BlockSpec((EXPERTS_PER_GROUP, D_MODEL, D_EXPERT), lambda m, g: (g, 0, 0)),
            pl.BlockSpec((EXPERTS_PER_GROUP, D_EXPERT, D_MODEL), lambda m, g: (g, 0, 0)),
            pl.BlockSpec((1, D_MODEL), lambda m, g: (0, 0)),
        ],
        out_specs=row(D_MODEL, pipeline_mode=once),
        out_shape=jax.ShapeDtypeStruct((seq, D_MODEL), F32),
        scratch_shapes=[pltpu.VMEM((MOE_CAP, D_MODEL), BF16), pltpu.VMEM((MOE_CAP, D_MODEL), BF16),
                        pltpu.VMEM((MOE_CAP, LANES), F32), pltpu.VMEM((MOE_TM, LANES), F32),
                        pltpu.SMEM((2 * N_GROUPS,), jnp.int32)],
        compiler_params=pltpu.CompilerParams(dimension_semantics=("parallel", "arbitrary"),
                                             vmem_limit_bytes=MOE_VMEM_LIMIT),
        name="moe_final",
    )(u2, logits, h1, tri, w_gate, w_up, w_down, final_w)


def kernel(x, meta_tokens, norm1_w, w_in, lambda_q1, lambda_k1, lambda_q2, lambda_k2, da_subln_w, dn_conv_w,
           dn_A_log, dn_dt_bias, dn_norm_w, w_branch_attn, w_branch_dn, w_out, norm2_w, router_group_w,
           router_group_b, router_expert_w, router_expert_b, expert_w_gate, expert_w_up, expert_w_down,
           final_norm_w):
    assert x.shape[0] == 1 and x.shape[2] == D_MODEL and x.shape[1] % ATT_TQ == 0
    assert w_in.shape[0] == 1, "single layer: the block output at the meta rows is unused"
    seq = x.shape[1]
    lt = TILE + seq
    assert lt <= 256 * LANES, "key positions must split into two exact bf16 terms"
    first = TILE - N_META

    x2d = x[0]
    meta_tile = jnp.concatenate([jnp.zeros((first, D_MODEL), F32), meta_tokens.astype(F32)], axis=0)

    w = w_in[0]
    d = D_MODEL
    w_a = w[:, 3 * d:6 * d].astype(BF16)
    w_b = jnp.concatenate([w[:, d:2 * d], w[:, 6 * d:7 * d], w[:, 7 * d + 2 * N_HEADS:]], axis=1).astype(BF16)
    w_qv_t = jnp.concatenate([w[:, :d] * DA_HEAD_DIM ** -0.5, w[:, 2 * d:3 * d]], axis=1).T.astype(BF16)
    w_small = jnp.pad(w[:, 7 * d:7 * d + 2 * N_HEADS], ((0, 0), (0, LANES - 2 * N_HEADS)))
    pad16 = lambda t, off: jnp.pad(t, ((0, 0), (off, LANES - N_HEADS - off)))
    proj, qvt, qn, kn, vc, bg = _inproj(x2d, meta_tile, norm1_w, w_a, w_b, w_qv_t, w_small, dn_conv_w[0],
                                        pad16(dn_A_log, N_HEADS), pad16(dn_dt_bias, N_HEADS), first, lt)

    pos = jnp.arange(lt, dtype=jnp.int32)
    kpos = jnp.stack([(pos >> 7) << 7, pos & 127, jnp.ones_like(pos)], axis=1).astype(BF16)
    kpos = jnp.pad(kpos, ((0, 0), (0, LANES - 3)))
    slopes = jnp.broadcast_to(jnp.asarray(ALIBI_SLOPES, F32)[:, None], (N_HEADS, LANES))
    lam_tab = jnp.pad(jnp.concatenate([lambda_q1, lambda_k1, lambda_q2, lambda_k2], axis=0),
                      ((0, 4), (0, LANES - DA_HEAD_DIM)))
    y_a = _attention(qvt, proj, kpos, slopes, lam_tab, da_subln_w, seq, first)

    y_d = _dn_chunks(qn, kn, vc, bg, proj, dn_norm_w)

    w_r = jnp.pad(jnp.concatenate([router_group_w[0], router_expert_w[0]], axis=1),
                  ((0, 0), (0, LANES - N_GROUPS - N_EXPERTS)))
    b_r = jnp.pad(jnp.concatenate([router_group_b, router_expert_b], axis=1),
                  ((0, 0), (0, LANES - N_GROUPS - N_EXPERTS)))
    h1, u2, logits = _mix(y_a, y_d, proj, x2d, w_branch_attn[0].astype(BF16), w_branch_dn[0].astype(BF16),
                          w_out[0].astype(BF16), norm2_w, w_r, b_r)

    tri = jnp.tril(jnp.ones((MOE_TM, MOE_TM), BF16), -1)
    out = _moe(u2, logits, h1, tri, expert_w_gate[0].astype(BF16), expert_w_up[0].astype(BF16),
               expert_w_down[0].astype(BF16), final_norm_w[None, :])
    return out[None]
```

```python
import functools
import math

import jax
import jax.numpy as jnp
from jax import lax
from jax.experimental import pallas as pl
from jax.experimental.pallas import tpu as pltpu

F32 = jnp.float32
BF16 = jnp.bfloat16

D_MODEL = 1024
N_META = 16
NORM_EPS = 1e-6
N_HEADS = 8
HEAD_W = 128
DA_HEAD_DIM = 64
DN_CONV = 4
DN_CHUNK = 64
N_GROUPS = 4
EXPERTS_PER_GROUP = 8
N_EXPERTS = 32
D_EXPERT = 256
LAMBDA_INIT = 0.8 - 0.6 * math.exp(-0.3 * 0)
ALIBI_SLOPES = tuple(2.0 ** (-8.0 * (h + 1) / N_HEADS) for h in range(N_HEADS))

TILE = 512
ATT_TQ = 2048
Q_CHAIN = 256
V_ROWS = 144
DN_ROWS = 256
DN_HEADS_PER_STEP = 8
MIX_SUB = 2
LANES = 128
NEG = -1e30
VMEM_LIMIT = 56 * 1024 * 1024
MOE_TM = 1024
MOE_CHUNK = 128
MOE_CAP = MOE_TM + N_GROUPS * MOE_CHUNK
MOE_VMEM_LIMIT = 60 * 1024 * 1024

CB_AK, CB_DG, CB_GA, CB_GD = 0, 8, 16, 24
PROJ_COLS = 32 * 128


def _cparams(sem):
    return pltpu.CompilerParams(dimension_semantics=sem, vmem_limit_bytes=VMEM_LIMIT)


def _mm(a, b):
    return jnp.dot(a.astype(BF16), b.astype(BF16), preferred_element_type=F32)


def _mm_nt(a, b):
    return lax.dot_general(a.astype(BF16), b.astype(BF16), (((1,), (1,)), ((), ())),
                           preferred_element_type=F32)


def _mm_tn(a, b):
    return lax.dot_general(a.astype(BF16), b.astype(BF16), (((0,), (0,)), ((), ())),
                           preferred_element_type=F32)


def _split_bf16(x):
    hi = x.astype(BF16)
    return hi, (x - hi.astype(F32)).astype(BF16)


def _mm_f32(a, b):
    a_hi, a_lo = _split_bf16(a)
    b_hi, b_lo = _split_bf16(b)
    dot = functools.partial(jnp.dot, preferred_element_type=F32)
    return dot(a_hi, b_hi) + (dot(a_hi, b_lo) + dot(a_lo, b_hi))


def _silu(x):
    return x * (1.0 / (1.0 + jnp.exp(-x)))


def _sigmoid(x):
    return 1.0 / (1.0 + jnp.exp(-x))


def _inproj_kernel(x_ref, meta_ref, nw_ref, wa_ref, wb_ref, wt_ref, ws_ref, cw_ref, al_ref, dtb_ref,
                   o_ref, ot_ref, q_ref, k_ref, v_ref, bg_ref, xe_scr, *, first_row, last_row):
    m = pl.program_id(0)
    tm = TILE

    @pl.when(m == 0)
    def _():
        xe_scr[0:8, :] = jnp.zeros((8, xe_scr.shape[1]), F32)

    x = jnp.where(m == 0, meta_ref[...], x_ref[...])
    u = x * lax.rsqrt(jnp.mean(x * x, axis=-1, keepdims=True) + NORM_EPS) * nw_ref[...]
    ub = u.astype(BF16)
    xe_scr[8:, :] = jnp.dot(ub, wa_ref[...], preferred_element_type=F32)
    o_ref[...] = jnp.dot(ub, wb_ref[...], preferred_element_type=F32).astype(BF16)
    ot_ref[0] = lax.dot_general(wt_ref[...], ub, (((1,), (1,)), ((), ())),
                                preferred_element_type=F32).astype(BF16)
    sm = _mm_f32(u, ws_ref[...])

    cw = cw_ref[...]
    acc = xe_scr[8:, :] * cw[DN_CONV - 1:DN_CONV, :]
    for j in range(DN_CONV - 1):
        acc = acc + xe_scr[pl.ds(8 - (DN_CONV - 1) + j, tm), :] * cw[j:j + 1, :]
    xe_scr[0:8, :] = xe_scr[tm:tm + 8, :]
    y = _silu(acc)

    width = N_HEADS * HEAD_W
    for h in range(N_HEADS):
        qh = y[:, h * HEAD_W:(h + 1) * HEAD_W]
        kh = y[:, width + h * HEAD_W: width + (h + 1) * HEAD_W]
        vh = y[:, 2 * width + h * HEAD_W: 2 * width + (h + 1) * HEAD_W]
        qh = qh * lax.rsqrt(jnp.sum(qh * qh, axis=-1, keepdims=True) + 1e-6) * (HEAD_W ** -0.5)
        kh = kh * lax.rsqrt(jnp.sum(kh * kh, axis=-1, keepdims=True) + 1e-6)
        q_ref[h] = qh.astype(BF16)
        k_ref[h] = kh.astype(BF16)
        v_ref[h] = vh.astype(BF16)

    rowid = m * tm + lax.broadcasted_iota(jnp.int32, (tm, LANES), 0)
    lane = lax.broadcasted_iota(jnp.int32, (tm, LANES), 1)
    real = (rowid >= first_row) & (rowid < last_row)
    a = sm + dtb_ref[...]
    softplus = jnp.maximum(a, 0.0) + jnp.log(1.0 + jnp.exp(-jnp.abs(a)))
    g_all = -jnp.exp(al_ref[...]) * softplus
    bg_ref[...] = jnp.where(real, jnp.where(lane < N_HEADS, _sigmoid(sm), g_all), 0.0)


def _inproj(x2d, meta_tile, norm_w, w_a, w_b, w_qv_t, w_small, conv_w, a_log_row, dtb_row, first_row, last_row):
    lt = TILE + x2d.shape[0]
    n_t = lt // TILE
    c = 3 * N_HEADS * HEAD_W
    once = pl.Buffered(1)
    const = lambda shape: pl.BlockSpec(shape, lambda m: (0,) * len(shape), pipeline_mode=once)
    kern = functools.partial(_inproj_kernel, first_row=first_row, last_row=last_row)
    hshape = jax.ShapeDtypeStruct((N_HEADS, lt, HEAD_W), BF16)
    hspec = pl.BlockSpec((N_HEADS, TILE, HEAD_W), lambda m: (0, m, 0))
    return pl.pallas_call(
        kern,
        grid=(n_t,),
        in_specs=[
            pl.BlockSpec((TILE, D_MODEL), lambda m: (jnp.maximum(m - 1, 0), 0)),
            const((TILE, D_MODEL)), const((1, D_MODEL)), const((D_MODEL, c)), const((D_MODEL, PROJ_COLS)),
            const((2 * D_MODEL, D_MODEL)), const((D_MODEL, LANES)), const((DN_CONV, c)),
            const((1, LANES)), const((1, LANES)),
        ],
        out_specs=[
            pl.BlockSpec((TILE, PROJ_COLS), lambda m: (m, 0)),
            pl.BlockSpec((1, 2 * D_MODEL, TILE), lambda m: (m, 0, 0)),
            hspec, hspec, hspec,
            pl.BlockSpec((TILE, LANES), lambda m: (m, 0)),
        ],
        out_shape=[
            jax.ShapeDtypeStruct((lt, PROJ_COLS), BF16),
            jax.ShapeDtypeStruct((n_t, 2 * D_MODEL, TILE), BF16),
            hshape, hshape, hshape,
            jax.ShapeDtypeStruct((lt, LANES), F32),
        ],
        scratch_shapes=[pltpu.VMEM((TILE + 8, c), F32)],
        compiler_params=pltpu.CompilerParams(dimension_semantics=("arbitrary",), vmem_limit_bytes=MOE_VMEM_LIMIT),
        name="inproj",
    )(x2d, meta_tile, norm_w, w_a, w_b, w_qv_t, w_small, conv_w, a_log_row, dtb_row)


def _attn_kernel(*refs, tq, first_key):
    tpq = tq // TILE
    q_refs = refs[:tpq]
    k_ref, kpos_ref, vt_ref, slope_ref, lam_ref, sw_ref, o_ref, m_ref, a_ref, s_ref = refs[tpq:]
    h = pl.program_id(0)
    j = pl.program_id(1)
    tk = TILE
    n_q = tq // Q_CHAIN

    qt = jnp.concatenate([r[0] for r in q_refs], axis=1)
    slope = slope_ref[pl.ds(h, 1), :][:, :1]
    qbase = TILE + j * tq
    rowi = lax.broadcasted_iota(jnp.int32, (HEAD_W, tq), 0)
    brow = jnp.where(rowi < 2, slope, jnp.where(rowi == 2, -(slope * qbase.astype(F32)), 0.0)).astype(BF16)
    z64 = jnp.zeros((DA_HEAD_DIM, tq), BF16)
    w1 = jnp.concatenate([qt[:DA_HEAD_DIM], z64, brow], axis=0)
    w2 = jnp.concatenate([z64, qt[DA_HEAD_DIM:], brow], axis=0)
    chains = [(w, hf) for w in (w1, w2) for hf in range(n_q)]
    ones_rows = jnp.where(lax.broadcasted_iota(jnp.int32, (V_ROWS - HEAD_W, tk), 0) == 0, 1.0, 0.0).astype(BF16)

    m_ref[...] = jnp.full(m_ref.shape, NEG, F32)
    a_ref[...] = jnp.zeros(a_ref.shape, F32)

    def stage(ki, modes, lo=0, rows=tk):
        start = pl.multiple_of(ki * tk, tk) + lo
        kt = jnp.concatenate([k_ref[pl.ds(start, rows), :], kpos_ref[pl.ds(start, rows), :]], axis=1)
        vt = jnp.concatenate([vt_ref[ki][:, lo:lo + rows], ones_rows[:, :rows]], axis=0)
        key = start + lax.broadcasted_iota(jnp.int32, (rows, Q_CHAIN), 0)
        lane = lax.broadcasted_iota(jnp.int32, (rows, Q_CHAIN), 1)
        tile_max = {}

        def score(c):
            w, hf = chains[c]
            st = jnp.dot(kt, w[:, hf * Q_CHAIN:(hf + 1) * Q_CHAIN], preferred_element_type=F32)
            if modes[hf] == "front":
                st = jnp.where(key >= first_key, st, NEG)
            elif modes[hf] == "causal":
                st = jnp.where(key <= lane + (qbase + hf * Q_CHAIN), st, NEG)
            s_ref[c, :rows] = st
            tile_max[c] = jnp.max(st, axis=0, keepdims=True)

        def accumulate(c):
            m_old = m_ref[c]
            m_new = jnp.maximum(m_old, tile_max[c])
            alpha = jnp.exp(m_old - m_new)
            p = jnp.exp(s_ref[c, :rows] - m_new).astype(BF16)
            a_ref[c] = alpha * a_ref[c] + jnp.dot(vt, p, preferred_element_type=F32)
            m_ref[c] = m_new

        live = [c for c, (_, hf) in enumerate(chains) if modes[hf] != "skip"]
        return live, score, accumulate

    def tile(ki, modes, **kw):
        live, score, accumulate = stage(ki, modes, **kw)
        for c in live:
            score(c)
        for c in live:
            accumulate(c)

    tile(0, ["front"] * n_q, lo=tk - LANES, rows=LANES)

    def body(ki, carry):
        tile(ki, ["none"] * n_q)
        return carry

    lax.fori_loop(1, tpq * j + 1, body, 0)
    for d in range(tpq):
        modes = ["skip" if (hf + 1) * Q_CHAIN <= d * tk else "none" if hf * Q_CHAIN >= (d + 1) * tk else "causal"
                 for hf in range(n_q)]
        tile(tpq * j + 1 + d, modes)

    lt = lam_ref[...]
    lam = (jnp.exp(jnp.sum(lt[0:1] * lt[1:2], axis=-1, keepdims=True))
           - jnp.exp(jnp.sum(lt[2:3] * lt[3:4], axis=-1, keepdims=True)) + LAMBDA_INIT)
    for hf in range(n_q):
        a1 = a_ref[hf]
        a2 = a_ref[n_q + hf]
        ot = a1[:HEAD_W] / a1[HEAD_W:HEAD_W + 1] - lam * (a2[:HEAD_W] / a2[HEAD_W:HEAD_W + 1])
        o = ot.T
        o = o * lax.rsqrt(jnp.mean(o * o, axis=-1, keepdims=True) + NORM_EPS) * sw_ref[...]
        o_ref[hf * Q_CHAIN:(hf + 1) * Q_CHAIN, :] = (o * (1.0 - LAMBDA_INIT)).astype(BF16)


def _attention(qvt, proj, kpos, slopes, lam_tab, subln_w, seq, first_key):
    n_t = qvt.shape[0]
    lt = n_t * TILE
    tq = ATT_TQ
    tpq = tq // TILE
    assert tq % TILE == 0 and seq % tq == 0 and first_key >= TILE - LANES
    kern = functools.partial(_attn_kernel, tq=tq, first_key=first_key)
    n_chain = 2 * tq // Q_CHAIN
    scratch = [pltpu.VMEM((n_chain, 1, Q_CHAIN), F32), pltpu.VMEM((n_chain, V_ROWS, Q_CHAIN), F32),
               pltpu.VMEM((n_chain, TILE, Q_CHAIN), F32)]
    q_spec = lambda d: pl.BlockSpec((1, HEAD_W, TILE), lambda h, j: (1 + d + tpq * j, h, 0))
    once = pl.Buffered(1)
    return pl.pallas_call(
        kern,
        grid=(N_HEADS, seq // tq),
        in_specs=[q_spec(d) for d in range(tpq)] + [
            pl.BlockSpec((lt, HEAD_W), lambda h, j: (0, CB_AK + h), pipeline_mode=once),
            pl.BlockSpec((lt, LANES), lambda h, j: (0, 0), pipeline_mode=once),
            pl.BlockSpec((n_t, HEAD_W, TILE), lambda h, j: (0, N_HEADS + h, 0), pipeline_mode=once),
            pl.BlockSpec((8, LANES), lambda h, j: (0, 0)),
            pl.BlockSpec((8, LANES), lambda h, j: (0, 0)),
            pl.BlockSpec((1, HEAD_W), lambda h, j: (0, 0)),
        ],
        out_specs=pl.BlockSpec((tq, HEAD_W), lambda h, j: (j, h)),
        out_shape=jax.ShapeDtypeStruct((seq, N_HEADS * HEAD_W), BF16),
        scratch_shapes=scratch,
        compiler_params=_cparams(("parallel", "arbitrary")),
        name="diff_attn",
    )(*([qvt] * tpq), proj, kpos, qvt, slopes, lam_tab, subln_w)


def _dn_chunk_kernel(q_ref, k_ref, v_ref, bg_ref, gate_ref, nw_ref, o_ref, s_scr):
    @pl.when(pl.program_id(1) == 0)
    def _():
        s_scr[...] = jnp.zeros(s_scr.shape, F32)

    r = DN_ROWS
    c = DN_CHUNK
    n_chunks = r // c
    hs = range(q_ref.shape[0])
    h0 = pl.program_id(0) * q_ref.shape[0]

    row = lax.broadcasted_iota(jnp.int32, (r, r), 0)
    col = lax.broadcasted_iota(jnp.int32, (r, r), 1)
    same64 = (row // c) == (col // c)
    same32 = (row // 32) == (col // 32)
    same16 = (row // 16) == (col // 16)
    incl = same64 & (row >= col)
    strict = same64 & (row > col)
    off16 = same32 & jnp.logical_not(same16)
    off32 = same64 & jnp.logical_not(same32)
    zero = jnp.zeros((r, r), F32)
    incl_b = jnp.where(incl, 1.0, 0.0).astype(BF16)

    q = [q_ref[h].astype(F32) for h in hs]
    k = [k_ref[h].astype(F32) for h in hs]
    v = [v_ref[h].astype(F32) for h in hs]
    bg = bg_ref[...]
    lane = lax.broadcasted_iota(jnp.int32, bg.shape, 1)
    pick = lambda l: jnp.broadcast_to(jnp.sum(jnp.where(lane == l, bg, 0.0), axis=-1, keepdims=True), bg.shape)
    beta = [pick(h0 + h) for h in hs]
    g_in = [pick(N_HEADS + h0 + h) for h in hs]

    def cumsum(g):
        g_hi = g.astype(BF16)
        r1 = g - g_hi.astype(F32)
        g_mid = r1.astype(BF16)
        g_lo = (r1 - g_mid.astype(F32)).astype(BF16)
        parts = jnp.dot(incl_b, jnp.concatenate([g_hi, g_mid, g_lo], axis=1), preferred_element_type=F32)
        return (parts[:, :LANES] + parts[:, LANES:2 * LANES]) + parts[:, 2 * LANES:]

    gc = [cumsum(g_in[h]) for h in hs]
    gc_row = [gc[h].T[0:1, :] for h in hs]
    diff = [jnp.concatenate([gc[h], gc[h]], axis=1) - gc_row[h] for h in hs]
    decay = [jnp.where(incl, jnp.exp(jnp.where(incl, diff[h], zero)), zero) for h in hs]
    gtot = [jnp.concatenate([jnp.broadcast_to(gc[h][(i + 1) * c - 1:(i + 1) * c, :], (c, LANES))
                             for i in range(n_chunks)], axis=0) for h in hs]
    egc = [jnp.exp(gc[h]) for h in hs]
    kb = [k[h] * beta[h] for h in hs]
    a = [jnp.where(strict, _mm_nt(kb[h], k[h]) * decay[h], zero).astype(BF16) for h in hs]

    zero_b = jnp.zeros((r, r), BF16)
    diag = row == col
    plus_eye = lambda p: jnp.where(diag, 1.0, p).astype(BF16)
    ad = [jnp.where(same16, a[h], zero_b) for h in hs]
    p2 = [_mm(ad[h], ad[h]).astype(BF16) for h in hs]
    p4 = [_mm(p2[h], p2[h]).astype(BF16) for h in hs]
    p8 = [_mm(p4[h], p4[h]) for h in hs]
    t = [_mm(plus_eye(-ad[h]), plus_eye(p2[h])).astype(BF16) for h in hs]
    t = [_mm(t[h], plus_eye(p4[h])).astype(BF16) for h in hs]
    t = [_mm(t[h], plus_eye(p8[h])) for h in hs]
    x = [_mm(t[h], jnp.where(off16, a[h], zero_b)) for h in hs]
    t = [t[h] - _mm(x[h], t[h]) for h in hs]
    x = [_mm(t[h], jnp.where(off32, a[h], zero_b)) for h in hs]
    t = [t[h] - _mm(x[h], t[h]) for h in hs]

    uw = [_mm(t[h], jnp.concatenate([v[h] * beta[h], kb[h] * egc[h]], axis=1)) for h in hs]
    qk = [jnp.where(incl, _mm_nt(q[h], k[h]) * decay[h], zero) for h in hs]
    qd = [q[h] * egc[h] for h in hs]
    kd = [k[h] * jnp.exp(gtot[h] - gc[h]) for h in hs]
    gts = [jnp.exp(gtot[h]) for h in hs]

    s = [s_scr[h] for h in hs]
    vns = [[] for _ in hs]
    outs = [[] for _ in hs]
    for i in range(n_chunks):
        rs = slice(i * c, (i + 1) * c)
        vn = [uw[h][rs, :HEAD_W] - _mm(uw[h][rs, HEAD_W:], s[h]) for h in hs]
        for h in hs:
            vns[h].append(vn[h])
        pad = [jnp.zeros(((n_chunks - 1 - i) * c, HEAD_W), F32)] * (i < n_chunks - 1)
        vn_all = [jnp.concatenate(vns[h] + pad, axis=0) for h in hs]
        o_i = [_mm(qd[h][rs], s[h]) + _mm(qk[h][rs], vn_all[h]) for h in hs]
        for h in hs:
            outs[h].append(o_i[h])
        s = [s[h] * gts[h][i * c:i * c + 1, :] + _mm_tn(kd[h][rs], vn[h]) for h in hs]
    for h in hs:
        s_scr[h] = s[h]

    for h in hs:
        o = jnp.concatenate(outs[h], axis=0)
        o = o * lax.rsqrt(jnp.mean(o * o, axis=-1, keepdims=True) + NORM_EPS) * nw_ref[...]
        gate = gate_ref[:, h * HEAD_W:(h + 1) * HEAD_W].astype(F32)
        o_ref[:, h * HEAD_W:(h + 1) * HEAD_W] = (o * _silu(gate)).astype(BF16)


def _dn_chunks(qn, kn, vc, bg, proj, norm_w):
    lt = qn.shape[1]
    hg = DN_HEADS_PER_STEP
    first_blk = (TILE - DN_ROWS) // DN_ROWS
    x_blk = TILE // DN_ROWS - first_blk
    hspec = pl.BlockSpec((hg, DN_ROWS, HEAD_W), lambda g, i: (g, i + first_blk, 0))
    wide = hg * HEAD_W
    return pl.pallas_call(
        _dn_chunk_kernel,
        grid=(N_HEADS // hg, lt // DN_ROWS - first_blk),
        in_specs=[
            hspec, hspec, hspec,
            pl.BlockSpec((DN_ROWS, LANES), lambda g, i: (i + first_blk, 0)),
            pl.BlockSpec((DN_ROWS, wide), lambda g, i: (i + first_blk, CB_DG * HEAD_W // wide + g)),
            pl.BlockSpec((1, HEAD_W), lambda g, i: (0, 0)),
        ],
        out_specs=pl.BlockSpec((DN_ROWS, wide), lambda g, i: (jnp.maximum(i - x_blk, 0), g)),
        out_shape=jax.ShapeDtypeStruct((lt - TILE, N_HEADS * HEAD_W), BF16),
        scratch_shapes=[pltpu.VMEM((hg, HEAD_W, HEAD_W), F32)],
        compiler_params=_cparams(("parallel", "arbitrary")),
        name="dn_chunks",
    )(qn, kn, vc, bg, proj, norm_w)


def _mix_kernel(ya_ref, yd_ref, ga_ref, gd_ref, h_ref, wa_ref, wd_ref, wo_ref, nw_ref, wr_ref, br_ref,
                h1_ref, u2_ref, lg_ref):
    n_sub = MIX_SUB
    rs = [slice(i * (TILE // n_sub), (i + 1) * (TILE // n_sub)) for i in range(n_sub)]
    pa = [jnp.dot(ya_ref[r, :], wa_ref[...], preferred_element_type=F32) for r in rs]
    pd = [jnp.dot(yd_ref[r, :], wd_ref[...], preferred_element_type=F32) for r in rs]
    mixed = [_sigmoid(ga_ref[r, :].astype(F32)) * pa[i] + _sigmoid(gd_ref[r, :].astype(F32)) * pd[i]
             for i, r in enumerate(rs)]
    h1 = [h_ref[r, :] + jnp.dot(mixed[i].astype(BF16), wo_ref[...], preferred_element_type=F32)
          for i, r in enumerate(rs)]
    u2 = [h * lax.rsqrt(jnp.mean(h * h, axis=-1, keepdims=True) + NORM_EPS) * nw_ref[...] for h in h1]
    for i, r in enumerate(rs):
        h1_ref[r, :] = h1[i]
        u2_ref[r, :] = u2[i].astype(BF16)
        lg_ref[r, :] = _mm_f32(u2[i], wr_ref[...]) + br_ref[...]


def _mix(y_a, y_d, proj, x2d, w_a, w_d, w_o, norm_w, w_r, b_r):
    seq = y_a.shape[0]
    skip = (proj.shape[0] - seq) // TILE
    row = lambda: pl.BlockSpec((TILE, D_MODEL), lambda m: (m, 0))
    prow = lambda cb: pl.BlockSpec((TILE, D_MODEL), lambda m: (m + skip, cb))
    full = lambda: pl.BlockSpec((D_MODEL, D_MODEL), lambda m: (0, 0))
    return pl.pallas_call(
        _mix_kernel,
        grid=(seq // TILE,),
        in_specs=[
            row(), row(), prow(CB_GA // 8), prow(CB_GD // 8), row(), full(), full(), full(),
            pl.BlockSpec((1, D_MODEL), lambda m: (0, 0)),
            pl.BlockSpec((D_MODEL, LANES), lambda m: (0, 0)),
            pl.BlockSpec((1, LANES), lambda m: (0, 0)),
        ],
        out_specs=[row(), row(), pl.BlockSpec((TILE, LANES), lambda m: (m, 0))],
        out_shape=[
            jax.ShapeDtypeStruct((seq, D_MODEL), F32),
            jax.ShapeDtypeStruct((seq, D_MODEL), BF16),
            jax.ShapeDtypeStruct((seq, LANES), F32),
        ],
        compiler_params=_cparams(("parallel",)),
        name="branch_mix",
    )(y_a, y_d, proj, proj, x2d, w_a, w_d, w_o, norm_w, w_r, b_r)


def _route(logits):
    shape = logits.shape
    lane = lax.broadcasted_iota(jnp.int32, shape, 1)
    big = jnp.full(shape, 4 * LANES, jnp.int32)
    neg = jnp.full(shape, NEG, F32)
    zero = jnp.zeros(shape, F32)
    is_g = lane < N_GROUPS
    gl = jnp.where(is_g, logits, neg)
    gmax = jnp.max(gl, axis=-1, keepdims=True)
    gsum = jnp.sum(jnp.where(is_g, jnp.exp(gl - gmax), zero), axis=-1, keepdims=True)
    g_w = 1.0 / gsum
    g_i = jnp.min(jnp.where(is_g & (gl == gmax), lane, big), axis=-1, keepdims=True)
    e_lane = lane - N_GROUPS
    in_grp = (e_lane >= 0) & (e_lane < N_EXPERTS) & ((e_lane // EXPERTS_PER_GROUP) == g_i)
    el = jnp.where(in_grp, logits, neg)
    emax = jnp.max(el, axis=-1, keepdims=True)
    eexp = jnp.where(in_grp, jnp.exp(el - emax), zero)
    ep = eexp / jnp.sum(eexp, axis=-1, keepdims=True)
    minus = jnp.full(shape, -1.0, F32)
    ep1 = jnp.where(in_grp, ep, minus)
    p1 = jnp.max(ep1, axis=-1, keepdims=True)
    i1 = jnp.min(jnp.where(ep1 == p1, lane, big), axis=-1, keepdims=True)
    ep2 = jnp.where(lane == i1, minus, ep1)
    p2 = jnp.max(ep2, axis=-1, keepdims=True)
    i2 = jnp.min(jnp.where(ep2 == p2, lane, big), axis=-1, keepdims=True)
    den = p1 + p2
    cmb = g_w * (jnp.where(lane == i1, p1 / den, zero) + jnp.where(lane == i2, p2 / den, zero))
    return cmb, jnp.where(lane == g_i, 1.0, 0.0)


def _moe_kernel(u_ref, lg_ref, h1_ref, tri_ref, wg_ref, wu_ref, wd_ref, fw_ref, o_ref,
                xs_scr, ys_scr, c8_scr, pos_scr, meta_ref):
    g = pl.program_id(1)
    tm = MOE_TM

    @pl.when(g == 0)
    def _():
        cmb, gsel = _route(lg_ref[...])
        lane1 = lax.broadcasted_iota(jnp.int32, (1, LANES), 1)
        rank = jnp.dot(tri_ref[...], gsel.astype(BF16), preferred_element_type=F32)
        cnt = jnp.sum(gsel, axis=0, keepdims=True)
        nch = jnp.floor((cnt + (MOE_CHUNK - 1.0)) * (1.0 / MOE_CHUNK))
        n0, n1, n2 = nch[:, 0:1], nch[:, 1:2], nch[:, 2:3]
        base = jnp.where(lane1 == 1, n0, jnp.where(lane1 == 2, n0 + n1, jnp.where(lane1 == 3, n0 + n1 + n2, 0.0)))
        for gi in range(N_GROUPS):
            meta_ref[gi] = jnp.sum(jnp.where(lane1 == gi, nch, 0.0)).astype(jnp.int32)
            meta_ref[N_GROUPS + gi] = jnp.sum(jnp.where(lane1 == gi, base, 0.0)).astype(jnp.int32)
        pos = jnp.sum(gsel * (rank + base * MOE_CHUNK), axis=-1, keepdims=True)
        pos_b = jnp.broadcast_to(pos, (tm, LANES))
        pos_scr[...] = pos_b
        pos_row = pos_b.T[0:1, :]
        rowi = lax.broadcasted_iota(jnp.int32, (MOE_CAP, tm), 0).astype(F32)
        perm = jnp.where(rowi == pos_row, 1.0, 0.0).astype(BF16)
        xs_scr[...] = jnp.dot(perm, u_ref[...], preferred_element_type=F32).astype(BF16)
        lrow = lax.broadcasted_iota(jnp.int32, (LANES, LANES), 0) - N_GROUPS
        lcol = lax.broadcasted_iota(jnp.int32, (LANES, LANES), 1)
        fold = jnp.where((lrow >= 0) & (lrow < N_EXPERTS) & ((lrow % EXPERTS_PER_GROUP) == lcol), 1.0, 0.0).astype(BF16)
        c_hi, c_lo = _split_bf16(cmb)
        c8 = jnp.dot(c_hi, fold, preferred_element_type=F32) + jnp.dot(c_lo, fold, preferred_element_type=F32)
        c8_hi, c8_lo = _split_bf16(c8)
        c8_scr[...] = (jnp.dot(perm, c8_hi, preferred_element_type=F32)
                       + jnp.dot(perm, c8_lo, preferred_element_type=F32))
        ys_scr[...] = jnp.zeros(ys_scr.shape, BF16)

    base_g = meta_ref[N_GROUPS + g]

    def chunk(jc, carry):
        r0 = pl.multiple_of((base_g + jc) * MOE_CHUNK, MOE_CHUNK)
        xc = xs_scr[pl.ds(r0, MOE_CHUNK), :]
        cw = c8_scr[pl.ds(r0, MOE_CHUNK), :]
        parts = []
        for e in range(EXPERTS_PER_GROUP):
            a = jnp.dot(xc, wg_ref[e], preferred_element_type=F32)
            b = jnp.dot(xc, wu_ref[e], preferred_element_type=F32)
            parts.append((_silu(a) * b * cw[:, e:e + 1]).astype(BF16))
        hh = jnp.concatenate(parts, axis=1)
        wd = wd_ref[...].reshape(EXPERTS_PER_GROUP * D_EXPERT, D_MODEL)
        ys_scr[pl.ds(r0, MOE_CHUNK), :] = jnp.dot(hh, wd, preferred_element_type=F32).astype(BF16)
        return carry

    lax.fori_loop(0, meta_ref[g], chunk, 0)

    @pl.when(g == N_GROUPS - 1)
    def _():
        pos_w = jnp.concatenate([pos_scr[...]] * (MOE_CAP // LANES), axis=1)
        lane = lax.broadcasted_iota(jnp.int32, (tm, MOE_CAP), 1).astype(F32)
        unperm = jnp.where(lane == pos_w, 1.0, 0.0).astype(BF16)
        hf = h1_ref[...] + jnp.dot(unperm, ys_scr[...], preferred_element_type=F32)
        o_ref[...] = hf * lax.rsqrt(jnp.mean(hf * hf, axis=-1, keepdims=True) + NORM_EPS) * fw_ref[...]


def _moe(u2, logits, h1, tri, w_gate, w_up, w_down, final_w):
    seq = u2.shape[0]
    once = pl.Buffered(1)
    row = lambda w, **kw: pl.BlockSpec((MOE_TM, w), lambda m, g: (m, 0), **kw)
    return pl.pallas_call(
        _moe_kernel,
        grid=(seq // MOE_TM, N_GROUPS),
        in_specs=[
            row(D_MODEL), row(LANES), row(D_MODEL, pipeline_mode=once),
            pl.BlockSpec((MOE_TM, MOE_TM), lambda m, g: (0, 0), pipeline_mode=once),
            pl.BlockSpec((EXPERTS_PER_GROUP, D_MODEL, D_EXPERT), lambda m, g: (g, 0, 0)),
            pl.BlockSpec((EXPERTS_PER_GROUP, D_MODEL, D_EXPERT), lambda m, g: (g, 0, 0)),
            pl.BlockSpec((EXPERTS_PER_GROUP, D_EXPERT, D_MODEL), lambda m, g: (g, 0, 0)),
            pl.BlockSpec((1, D_MODEL), lambda m, g: (0, 0)),
        ],
        out_specs=row(D_MODEL, pipeline_mode=once),
        out_shape=jax.ShapeDtypeStruct((seq, D_MODEL), F32),
        scratch_shapes=[pltpu.VMEM((MOE_CAP, D_MODEL), BF16), pltpu.VMEM((MOE_CAP, D_MODEL), BF16),
                        pltpu.VMEM((MOE_CAP, LANES), F32), pltpu.VMEM((MOE_TM, LANES), F32),
                        pltpu.SMEM((2 * N_GROUPS,), jnp.int32)],
        compiler_params=pltpu.CompilerParams(dimension_semantics=("parallel", "arbitrary"),
                                             vmem_limit_bytes=MOE_VMEM_LIMIT),
        name="moe_final",
    )(u2, logits, h1, tri, w_gate, w_up, w_down, final_w)


def kernel(x, meta_tokens, norm1_w, w_in, lambda_q1, lambda_k1, lambda_q2, lambda_k2, da_subln_w, dn_conv_w,
           dn_A_log, dn_dt_bias, dn_norm_w, w_branch_attn, w_branch_dn, w_out, norm2_w, router_group_w,
           router_group_b, router_expert_w, router_expert_b, expert_w_gate, expert_w_up, expert_w_down,
           final_norm_w):
    assert x.shape[0] == 1 and x.shape[2] == D_MODEL and x.shape[1] % ATT_TQ == 0
    assert w_in.shape[0] == 1, "single layer: the block output at the meta rows is unused"
    seq = x.shape[1]
    lt = TILE + seq
    assert lt <= 256 * LANES, "key positions must split into two exact bf16 terms"
    first = TILE - N_META

    x2d = x[0]
    meta_tile = jnp.concatenate([jnp.zeros((first, D_MODEL), F32), meta_tokens.astype(F32)], axis=0)

    w = w_in[0]
    d = D_MODEL
    w_a = w[:, 3 * d:6 * d].astype(BF16)
    w_b = jnp.concatenate([w[:, d:2 * d], w[:, 6 * d:7 * d], w[:, 7 * d + 2 * N_HEADS:]], axis=1).astype(BF16)
    w_qv_t = jnp.concatenate([w[:, :d] * DA_HEAD_DIM ** -0.5, w[:, 2 * d:3 * d]], axis=1).T.astype(BF16)
    w_small = jnp.pad(w[:, 7 * d:7 * d + 2 * N_HEADS], ((0, 0), (0, LANES - 2 * N_HEADS)))
    pad16 = lambda t, off: jnp.pad(t, ((0, 0), (off, LANES - N_HEADS - off)))
    proj, qvt, qn, kn, vc, bg = _inproj(x2d, meta_tile, norm1_w, w_a, w_b, w_qv_t, w_small, dn_conv_w[0],
                                        pad16(dn_A_log, N_HEADS), pad16(dn_dt_bias, N_HEADS), first, lt)

    pos = jnp.arange(lt, dtype=jnp.int32)
    kpos = jnp.stack([(pos >> 7) << 7, pos & 127, jnp.ones_like(pos)], axis=1).astype(BF16)
    kpos = jnp.pad(kpos, ((0, 0), (0, LANES - 3)))
    slopes = jnp.broadcast_to(jnp.asarray(ALIBI_SLOPES, F32)[:, None], (N_HEADS, LANES))
    lam_tab = jnp.pad(jnp.concatenate([lambda_q1, lambda_k1, lambda_q2, lambda_k2], axis=0),
                      ((0, 4), (0, LANES - DA_HEAD_DIM)))
    y_a = _attention(qvt, proj, kpos, slopes, lam_tab, da_subln_w, seq, first)

    y_d = _dn_chunks(qn, kn, vc, bg, proj, dn_norm_w)

    w_r = jnp.pad(jnp.concatenate([router_group_w[0], router_expert_w[0]], axis=1),
                  ((0, 0), (0, LANES - N_GROUPS - N_EXPERTS)))
    b_r = jnp.pad(jnp.concatenate([router_group_b, router_expert_b], axis=1),
                  ((0, 0), (0, LANES - N_GROUPS - N_EXPERTS)))
    h1, u2, logits = _mix(y_a, y_d, proj, x2d, w_branch_attn[0].astype(BF16), w_branch_dn[0].astype(BF16),
                          w_out[0].astype(BF16), norm2_w, w_r, b_r)

    tri = jnp.tril(jnp.ones((MOE_TM, MOE_TM), BF16), -1)
    out = _moe(u2, logits, h1, tri, expert_w_gate[0].astype(BF16), expert_w_up[0].astype(BF16),
               expert_w_down[0].astype(BF16), final_norm_w[None, :])
    return out[None]
```

```python
import functools
import math

import jax
import jax.numpy as jnp
from jax import lax
from jax.experimental import pallas as pl
from jax.experimental.pallas import tpu as pltpu

F32 = jnp.float32
BF16 = jnp.bfloat16

D_MODEL = 1024
N_META = 16
NORM_EPS = 1e-6
N_HEADS = 8
HEAD_W = 128
DA_HEAD_DIM = 64
DN_CONV = 4
DN_CHUNK = 64
N_GROUPS = 4
EXPERTS_PER_GROUP = 8
N_EXPERTS = 32
D_EXPERT = 256
LAMBDA_INIT = 0.8 - 0.6 * math.exp(-0.3 * 0)
ALIBI_SLOPES = tuple(2.0 ** (-8.0 * (h + 1) / N_HEADS) for h in range(N_HEADS))

TILE = 512
ATT_TQ = 2048
Q_CHAIN = 256
V_ROWS = 144
DN_ROWS = 256
DN_HEADS_PER_STEP = 8
MIX_SUB = 2
LANES = 128
NEG = -1e30
VMEM_LIMIT = 56 * 1024 * 1024
MOE_TM = 1024
MOE_CHUNK = 128
MOE_CAP = MOE_TM + N_GROUPS * MOE_CHUNK
MOE_VMEM_LIMIT = 60 * 1024 * 1024

CB_AK, CB_DG, CB_GA, CB_GD = 0, 8, 16, 24
PROJ_COLS = 32 * 128


def _cparams(sem):
    return pltpu.CompilerParams(dimension_semantics=sem, vmem_limit_bytes=VMEM_LIMIT)


def _mm(a, b):
    return jnp.dot(a.astype(BF16), b.astype(BF16), preferred_element_type=F32)


def _mm_nt(a, b):
    return lax.dot_general(a.astype(BF16), b.astype(BF16), (((1,), (1,)), ((), ())),
                           preferred_element_type=F32)


def _mm_tn(a, b):
    return lax.dot_general(a.astype(BF16), b.astype(BF16), (((0,), (0,)), ((), ())),
                           preferred_element_type=F32)


def _split_bf16(x):
    hi = x.astype(BF16)
    return hi, (x - hi.astype(F32)).astype(BF16)


def _mm_f32(a, b):
    a_hi, a_lo = _split_bf16(a)
    b_hi, b_lo = _split_bf16(b)
    dot = functools.partial(jnp.dot, preferred_element_type=F32)
    return dot(a_hi, b_hi) + (dot(a_hi, b_lo) + dot(a_lo, b_hi))


def _silu(x):
    return x * (1.0 / (1.0 + jnp.exp(-x)))


def _sigmoid(x):
    return 1.0 / (1.0 + jnp.exp(-x))


def _inproj_kernel(x_ref, meta_ref, nw_ref, wa_ref, wb_ref, wt_ref, ws_ref, cw_ref, al_ref, dtb_ref,
                   o_ref, ot_ref, q_ref, k_ref, v_ref, bg_ref, xe_scr, *, first_row, last_row):
    m = pl.program_id(0)
    tm = TILE

    @pl.when(m == 0)
    def _():
        xe_scr[0:8, :] = jnp.zeros((8, xe_scr.shape[1]), F32)

    x = jnp.where(m == 0, meta_ref[...], x_ref[...])
    u = x * lax.rsqrt(jnp.mean(x * x, axis=-1, keepdims=True) + NORM_EPS) * nw_ref[...]
    ub = u.astype(BF16)
    xe_scr[8:, :] = jnp.dot(ub, wa_ref[...], preferred_element_type=F32)
    o_ref[...] = jnp.dot(ub, wb_ref[...], preferred_element_type=F32).astype(BF16)
    ot_ref[0] = lax.dot_general(wt_ref[...], ub, (((1,), (1,)), ((), ())),
                                preferred_element_type=F32).astype(BF16)
    sm = _mm_f32(u, ws_ref[...])

    cw = cw_ref[...]
    acc = xe_scr[8:, :] * cw[DN_CONV - 1:DN_CONV, :]
    for j in range(DN_CONV - 1):
        acc = acc + xe_scr[pl.ds(8 - (DN_CONV - 1) + j, tm), :] * cw[j:j + 1, :]
    xe_scr[0:8, :] = xe_scr[tm:tm + 8, :]
    y = _silu(acc)

    width = N_HEADS * HEAD_W
    for h in range(N_HEADS):
        qh = y[:, h * HEAD_W:(h + 1) * HEAD_W]
        kh = y[:, width + h * HEAD_W: width + (h + 1) * HEAD_W]
        vh = y[:, 2 * width + h * HEAD_W: 2 * width + (h + 1) * HEAD_W]
        qh = qh * lax.rsqrt(jnp.sum(qh * qh, axis=-1, keepdims=True) + 1e-6) * (HEAD_W ** -0.5)
        kh = kh * lax.rsqrt(jnp.sum(kh * kh, axis=-1, keepdims=True) + 1e-6)
        q_ref[h] = qh.astype(BF16)
        k_ref[h] = kh.astype(BF16)
        v_ref[h] = vh.astype(BF16)

    rowid = m * tm + lax.broadcasted_iota(jnp.int32, (tm, LANES), 0)
    lane = lax.broadcasted_iota(jnp.int32, (tm, LANES), 1)
    real = (rowid >= first_row) & (rowid < last_row)
    a = sm + dtb_ref[...]
    softplus = jnp.maximum(a, 0.0) + jnp.log(1.0 + jnp.exp(-jnp.abs(a)))
    g_all = -jnp.exp(al_ref[...]) * softplus
    bg_ref[...] = jnp.where(real, jnp.where(lane < N_HEADS, _sigmoid(sm), g_all), 0.0)


def _inproj(x2d, meta_tile, norm_w, w_a, w_b, w_qv_t, w_small, conv_w, a_log_row, dtb_row, first_row, last_row):
    lt = TILE + x2d.shape[0]
    n_t = lt // TILE
    c = 3 * N_HEADS * HEAD_W
    once = pl.Buffered(1)
    const = lambda shape: pl.BlockSpec(shape, lambda m: (0,) * len(shape), pipeline_mode=once)
    kern = functools.partial(_inproj_kernel, first_row=first_row, last_row=last_row)
    hshape = jax.ShapeDtypeStruct((N_HEADS, lt, HEAD_W), BF16)
    hspec = pl.BlockSpec((N_HEADS, TILE, HEAD_W), lambda m: (0, m, 0))
    return pl.pallas_call(
        kern,
        grid=(n_t,),
        in_specs=[
            pl.BlockSpec((TILE, D_MODEL), lambda m: (jnp.maximum(m - 1, 0), 0)),
            const((TILE, D_MODEL)), const((1, D_MODEL)), const((D_MODEL, c)), const((D_MODEL, PROJ_COLS)),
            const((2 * D_MODEL, D_MODEL)), const((D_MODEL, LANES)), const((DN_CONV, c)),
            const((1, LANES)), const((1, LANES)),
        ],
        out_specs=[
            pl.BlockSpec((TILE, PROJ_COLS), lambda m: (m, 0)),
            pl.BlockSpec((1, 2 * D_MODEL, TILE), lambda m: (m, 0, 0)),
            hspec, hspec, hspec,
            pl.BlockSpec((TILE, LANES), lambda m: (m, 0)),
        ],
        out_shape=[
            jax.ShapeDtypeStruct((lt, PROJ_COLS), BF16),
            jax.ShapeDtypeStruct((n_t, 2 * D_MODEL, TILE), BF16),
            hshape, hshape, hshape,
            jax.ShapeDtypeStruct((lt, LANES), F32),
        ],
        scratch_shapes=[pltpu.VMEM((TILE + 8, c), F32)],
        compiler_params=pltpu.CompilerParams(dimension_semantics=("arbitrary",), vmem_limit_bytes=MOE_VMEM_LIMIT),
        name="inproj",
    )(x2d, meta_tile, norm_w, w_a, w_b, w_qv_t, w_small, conv_w, a_log_row, dtb_row)


def _attn_kernel(*refs, tq, first_key):
    tpq = tq // TILE
    q_refs = refs[:tpq]
    k_ref, kpos_ref, vt_ref, slope_ref, lam_ref, sw_ref, o_ref, m_ref, a_ref, s_ref = refs[tpq:]
    h = pl.program_id(0)
    j = pl.program_id(1)
    tk = TILE
    n_q = tq // Q_CHAIN

    qt = jnp.concatenate([r[0] for r in q_refs], axis=1)
    slope = slope_ref[pl.ds(h, 1), :][:, :1]
    qbase = TILE + j * tq
    rowi = lax.broadcasted_iota(jnp.int32, (HEAD_W, tq), 0)
    brow = jnp.where(rowi < 2, slope, jnp.where(rowi == 2, -(slope * qbase.astype(F32)), 0.0)).astype(BF16)
    z64 = jnp.zeros((DA_HEAD_DIM, tq), BF16)
    w1 = jnp.concatenate([qt[:DA_HEAD_DIM], z64, brow], axis=0)
    w2 = jnp.concatenate([z64, qt[DA_HEAD_DIM:], brow], axis=0)
    chains = [(w, hf) for w in (w1, w2) for hf in range(n_q)]
    ones_rows = jnp.where(lax.broadcasted_iota(jnp.int32, (V_ROWS - HEAD_W, tk), 0) == 0, 1.0, 0.0).astype(BF16)

    m_ref[...] = jnp.full(m_ref.shape, NEG, F32)
    a_ref[...] = jnp.zeros(a_ref.shape, F32)

    def stage(ki, modes, lo=0, rows=tk):
        start = pl.multiple_of(ki * tk, tk) + lo
        kt = jnp.concatenate([k_ref[pl.ds(start, rows), :], kpos_ref[pl.ds(start, rows), :]], axis=1)
        vt = jnp.concatenate([vt_ref[ki][:, lo:lo + rows], ones_rows[:, :rows]], axis=0)
        key = start + lax.broadcasted_iota(jnp.int32, (rows, Q_CHAIN), 0)
        lane = lax.broadcasted_iota(jnp.int32, (rows, Q_CHAIN), 1)
        tile_max = {}

        def score(c):
            w, hf = chains[c]
            st = jnp.dot(kt, w[:, hf * Q_CHAIN:(hf + 1) * Q_CHAIN], preferred_element_type=F32)
            if modes[hf] == "front":
                st = jnp.where(key >= first_key, st, NEG)
            elif modes[hf] == "causal":
                st = jnp.where(key <= lane + (qbase + hf * Q_CHAIN), st, NEG)
            s_ref[c, :rows] = st
            tile_max[c] = jnp.max(st, axis=0, keepdims=True)

        def accumulate(c):
            m_old = m_ref[c]
            m_new = jnp.maximum(m_old, tile_max[c])
            alpha = jnp.exp(m_old - m_new)
            p = jnp.exp(s_ref[c, :rows] - m_new).astype(BF16)
            a_ref[c] = alpha * a_ref[c] + jnp.dot(vt, p, preferred_element_type=F32)
            m_ref[c] = m_new

        live = [c for c, (_, hf) in enumerate(chains) if modes[hf] != "skip"]
        return live, score, accumulate

    def tile(ki, modes, **kw):
        live, score, accumulate = stage(ki, modes, **kw)
        for c in live:
            score(c)
        for c in live:
            accumulate(c)

    tile(0, ["front"] * n_q, lo=tk - LANES, rows=LANES)

    def body(ki, carry):
        tile(ki, ["none"] * n_q)
        return carry

    lax.fori_loop(1, tpq * j + 1, body, 0)
    for d in range(tpq):
        modes = ["skip" if (hf + 1) * Q_CHAIN <= d * tk else "none" if hf * Q_CHAIN >= (d + 1) * tk else "causal"
                 for hf in range(n_q)]
        tile(tpq * j + 1 + d, modes)

    lt = lam_ref[...]
    lam = (jnp.exp(jnp.sum(lt[0:1] * lt[1:2], axis=-1, keepdims=True))
           - jnp.exp(jnp.sum(lt[2:3] * lt[3:4], axis=-1, keepdims=True)) + LAMBDA_INIT)
    for hf in range(n_q):
        a1 = a_ref[hf]
        a2 = a_ref[n_q + hf]
        ot = a1[:HEAD_W] / a1[HEAD_W:HEAD_W + 1] - lam * (a2[:HEAD_W] / a2[HEAD_W:HEAD_W + 1])
        o = ot.T
        o = o * lax.rsqrt(jnp.mean(o * o, axis=-1, keepdims=True) + NORM_EPS) * sw_ref[...]
        o_ref[hf * Q_CHAIN:(hf + 1) * Q_CHAIN, :] = (o * (1.0 - LAMBDA_INIT)).astype(BF16)


def _attention(qvt, proj, kpos, slopes, lam_tab, subln_w, seq, first_key):
    n_t = qvt.shape[0]
    lt = n_t * TILE
    tq = ATT_TQ
    tpq = tq // TILE
    assert tq % TILE == 0 and seq % tq == 0 and first_key >= TILE - LANES
    kern = functools.partial(_attn_kernel, tq=tq, first_key=first_key)
    n_chain = 2 * tq // Q_CHAIN
    scratch = [pltpu.VMEM((n_chain, 1, Q_CHAIN), F32), pltpu.VMEM((n_chain, V_ROWS, Q_CHAIN), F32),
               pltpu.VMEM((n_chain, TILE, Q_CHAIN), F32)]
    q_spec = lambda d: pl.BlockSpec((1, HEAD_W, TILE), lambda h, j: (1 + d + tpq * j, h, 0))
    once = pl.Buffered(1)
    return pl.pallas_call(
        kern,
        grid=(N_HEADS, seq // tq),
        in_specs=[q_spec(d) for d in range(tpq)] + [
            pl.BlockSpec((lt, HEAD_W), lambda h, j: (0, CB_AK + h), pipeline_mode=once),
            pl.BlockSpec((lt, LANES), lambda h, j: (0, 0), pipeline_mode=once),
            pl.BlockSpec((n_t, HEAD_W, TILE), lambda h, j: (0, N_HEADS + h, 0), pipeline_mode=once),
            pl.BlockSpec((8, LANES), lambda h, j: (0, 0)),
            pl.BlockSpec((8, LANES), lambda h, j: (0, 0)),
            pl.BlockSpec((1, HEAD_W), lambda h, j: (0, 0)),
        ],
        out_specs=pl.BlockSpec((tq, HEAD_W), lambda h, j: (j, h)),
        out_shape=jax.ShapeDtypeStruct((seq, N_HEADS * HEAD_W), BF16),
        scratch_shapes=scratch,
        compiler_params=_cparams(("parallel", "arbitrary")),
        name="diff_attn",
    )(*([qvt] * tpq), proj, kpos, qvt, slopes, lam_tab, subln_w)


def _dn_chunk_kernel(q_ref, k_ref, v_ref, bg_ref, gate_ref, nw_ref, o_ref, s_scr):
    @pl.when(pl.program_id(1) == 0)
    def _():
        s_scr[...] = jnp.zeros(s_scr.shape, F32)

    r = DN_ROWS
    c = DN_CHUNK
    n_chunks = r // c
    hs = range(q_ref.shape[0])
    h0 = pl.program_id(0) * q_ref.shape[0]

    row = lax.broadcasted_iota(jnp.int32, (r, r), 0)
    col = lax.broadcasted_iota(jnp.int32, (r, r), 1)
    same64 = (row // c) == (col // c)
    same32 = (row // 32) == (col // 32)
    same16 = (row // 16) == (col // 16)
    incl = same64 & (row >= col)
    strict = same64 & (row > col)
    off16 = same32 & jnp.logical_not(same16)
    off32 = same64 & jnp.logical_not(same32)
    zero = jnp.zeros((r, r), F32)
    incl_b = jnp.where(incl, 1.0, 0.0).astype(BF16)

    q = [q_ref[h].astype(F32) for h in hs]
    k = [k_ref[h].astype(F32) for h in hs]
    v = [v_ref[h].astype(F32) for h in hs]
    bg = bg_ref[...]
    lane = lax.broadcasted_iota(jnp.int32, bg.shape, 1)
    pick = lambda l: jnp.broadcast_to(jnp.sum(jnp.where(lane == l, bg, 0.0), axis=-1, keepdims=True), bg.shape)
    beta = [pick(h0 + h) for h in hs]
    g_in = [pick(N_HEADS + h0 + h) for h in hs]

    def cumsum(g):
        g_hi = g.astype(BF16)
        r1 = g - g_hi.astype(F32)
        g_mid = r1.astype(BF16)
        g_lo = (r1 - g_mid.astype(F32)).astype(BF16)
        parts = jnp.dot(incl_b, jnp.concatenate([g_hi, g_mid, g_lo], axis=1), preferred_element_type=F32)
        return (parts[:, :LANES] + parts[:, LANES:2 * LANES]) + parts[:, 2 * LANES:]

    gc = [cumsum(g_in[h]) for h in hs]
    gc_row = [gc[h].T[0:1, :] for h in hs]
    diff = [jnp.concatenate([gc[h], gc[h]], axis=1) - gc_row[h] for h in hs]
    decay = [jnp.where(incl, jnp.exp(jnp.where(incl, diff[h], zero)), zero) for h in hs]
    gtot = [jnp.concatenate([jnp.broadcast_to(gc[h][(i + 1) * c - 1:(i + 1) * c, :], (c, LANES))
                             for i in range(n_chunks)], axis=0) for h in hs]
    egc = [jnp.exp(gc[h]) for h in hs]
    kb = [k[h] * beta[h] for h in hs]
    a = [jnp.where(strict, _mm_nt(kb[h], k[h]) * decay[h], zero).astype(BF16) for h in hs]

    zero_b = jnp.zeros((r, r), BF16)
    diag = row == col
    plus_eye = lambda p: jnp.where(diag, 1.0, p).astype(BF16)
    ad = [jnp.where(same16, a[h], zero_b) for h in hs]
    p2 = [_mm(ad[h], ad[h]).astype(BF16) for h in hs]
    p4 = [_mm(p2[h], p2[h]).astype(BF16) for h in hs]
    p8 = [_mm(p4[h], p4[h]) for h in hs]
    t = [_mm(plus_eye(-ad[h]), plus_eye(p2[h])).astype(BF16) for h in hs]
    t = [_mm(t[h], plus_eye(p4[h])).astype(BF16) for h in hs]
    t = [_mm(t[h], plus_eye(p8[h])) for h in hs]
    x = [_mm(t[h], jnp.where(off16, a[h], zero_b)) for h in hs]
    t = [t[h] - _mm(x[h], t[h]) for h in hs]
    x = [_mm(t[h], jnp.where(off32, a[h], zero_b)) for h in hs]
    t = [t[h] - _mm(x[h], t[h]) for h in hs]

    uw = [_mm(t[h], jnp.concatenate([v[h] * beta[h], kb[h] * egc[h]], axis=1)) for h in hs]
    qk = [jnp.where(incl, _mm_nt(q[h], k[h]) * decay[h], zero) for h in hs]
    qd = [q[h] * egc[h] for h in hs]
    kd = [k[h] * jnp.exp(gtot[h] - gc[h]) for h in hs]
    gts = [jnp.exp(gtot[h]) for h in hs]

    s = [s_scr[h] for h in hs]
    vns = [[] for _ in hs]
    outs = [[] for _ in hs]
    for i in range(n_chunks):
        rs = slice(i * c, (i + 1) * c)
        vn = [uw[h][rs, :HEAD_W] - _mm(uw[h][rs, HEAD_W:], s[h]) for h in hs]
        for h in hs:
            vns[h].append(vn[h])
        pad = [jnp.zeros(((n_chunks - 1 - i) * c, HEAD_W), F32)] * (i < n_chunks - 1)
        vn_all = [jnp.concatenate(vns[h] + pad, axis=0) for h in hs]
        o_i = [_mm(qd[h][rs], s[h]) + _mm(qk[h][rs], vn_all[h]) for h in hs]
        for h in hs:
            outs[h].append(o_i[h])
        s = [s[h] * gts[h][i * c:i * c + 1, :] + _mm_tn(kd[h][rs], vn[h]) for h in hs]
    for h in hs:
        s_scr[h] = s[h]

    for h in hs:
        o = jnp.concatenate(outs[h], axis=0)
        o = o * lax.rsqrt(jnp.mean(o * o, axis=-1, keepdims=True) + NORM_EPS) * nw_ref[...]
        gate = gate_ref[:, h * HEAD_W:(h + 1) * HEAD_W].astype(F32)
        o_ref[:, h * HEAD_W:(h + 1) * HEAD_W] = (o * _silu(gate)).astype(BF16)


def _dn_chunks(qn, kn, vc, bg, proj, norm_w):
    lt = qn.shape[1]
    hg = DN_HEADS_PER_STEP
    first_blk = (TILE - DN_ROWS) // DN_ROWS
    x_blk = TILE // DN_ROWS - first_blk
    hspec = pl.BlockSpec((hg, DN_ROWS, HEAD_W), lambda g, i: (g, i + first_blk, 0))
    wide = hg * HEAD_W
    return pl.pallas_call(
        _dn_chunk_kernel,
        grid=(N_HEADS // hg, lt // DN_ROWS - first_blk),
        in_specs=[
            hspec, hspec, hspec,
            pl.BlockSpec((DN_ROWS, LANES), lambda g, i: (i + first_blk, 0)),
            pl.BlockSpec((DN_ROWS, wide), lambda g, i: (i + first_blk, CB_DG * HEAD_W // wide + g)),
            pl.BlockSpec((1, HEAD_W), lambda g, i: (0, 0)),
        ],
        out_specs=pl.BlockSpec((DN_ROWS, wide), lambda g, i: (jnp.maximum(i - x_blk, 0), g)),
        out_shape=jax.ShapeDtypeStruct((lt - TILE, N_HEADS * HEAD_W), BF16),
        scratch_shapes=[pltpu.VMEM((hg, HEAD_W, HEAD_W), F32)],
        compiler_params=_cparams(("parallel", "arbitrary")),
        name="dn_chunks",
    )(qn, kn, vc, bg, proj, norm_w)


def _mix_kernel(ya_ref, yd_ref, ga_ref, gd_ref, h_ref, wa_ref, wd_ref, wo_ref, nw_ref, wr_ref, br_ref,
                h1_ref, u2_ref, lg_ref):
    n_sub = MIX_SUB
    rs = [slice(i * (TILE // n_sub), (i + 1) * (TILE // n_sub)) for i in range(n_sub)]
    pa = [jnp.dot(ya_ref[r, :], wa_ref[...], preferred_element_type=F32) for r in rs]
    pd = [jnp.dot(yd_ref[r, :], wd_ref[...], preferred_element_type=F32) for r in rs]
    mixed = [_sigmoid(ga_ref[r, :].astype(F32)) * pa[i] + _sigmoid(gd_ref[r, :].astype(F32)) * pd[i]
             for i, r in enumerate(rs)]
    h1 = [h_ref[r, :] + jnp.dot(mixed[i].astype(BF16), wo_ref[...], preferred_element_type=F32)
          for i, r in enumerate(rs)]
    u2 = [h * lax.rsqrt(jnp.mean(h * h, axis=-1, keepdims=True) + NORM_EPS) * nw_ref[...] for h in h1]
    for i, r in enumerate(rs):
        h1_ref[r, :] = h1[i]
        u2_ref[r, :] = u2[i].astype(BF16)
        lg_ref[r, :] = _mm_f32(u2[i], wr_ref[...]) + br_ref[...]


def _mix(y_a, y_d, proj, x2d, w_a, w_d, w_o, norm_w, w_r, b_r):
    seq = y_a.shape[0]
    skip = (proj.shape[0] - seq) // TILE
    row = lambda: pl.BlockSpec((TILE, D_MODEL), lambda m: (m, 0))
    prow = lambda cb: pl.BlockSpec((TILE, D_MODEL), lambda m: (m + skip, cb))
    full = lambda: pl.BlockSpec((D_MODEL, D_MODEL), lambda m: (0, 0))
    return pl.pallas_call(
        _mix_kernel,
        grid=(seq // TILE,),
        in_specs=[
            row(), row(), prow(CB_GA // 8), prow(CB_GD // 8), row(), full(), full(), full(),
            pl.BlockSpec((1, D_MODEL), lambda m: (0, 0)),
            pl.BlockSpec((D_MODEL, LANES), lambda m: (0, 0)),
            pl.BlockSpec((1, LANES), lambda m: (0, 0)),
        ],
        out_specs=[row(), row(), pl.BlockSpec((TILE, LANES), lambda m: (m, 0))],
        out_shape=[
            jax.ShapeDtypeStruct((seq, D_MODEL), F32),
            jax.ShapeDtypeStruct((seq, D_MODEL), BF16),
            jax.ShapeDtypeStruct((seq, LANES), F32),
        ],
        compiler_params=_cparams(("parallel",)),
        name="branch_mix",
    )(y_a, y_d, proj, proj, x2d, w_a, w_d, w_o, norm_w, w_r, b_r)


def _route(logits):
    shape = logits.shape
    lane = lax.broadcasted_iota(jnp.int32, shape, 1)
    big = jnp.full(shape, 4 * LANES, jnp.int32)
    neg = jnp.full(shape, NEG, F32)
    zero = jnp.zeros(shape, F32)
    is_g = lane < N_GROUPS
    gl = jnp.where(is_g, logits, neg)
    gmax = jnp.max(gl, axis=-1, keepdims=True)
    gsum = jnp.sum(jnp.where(is_g, jnp.exp(gl - gmax), zero), axis=-1, keepdims=True)
    g_w = 1.0 / gsum
    g_i = jnp.min(jnp.where(is_g & (gl == gmax), lane, big), axis=-1, keepdims=True)
    e_lane = lane - N_GROUPS
    in_grp = (e_lane >= 0) & (e_lane < N_EXPERTS) & ((e_lane // EXPERTS_PER_GROUP) == g_i)
    el = jnp.where(in_grp, logits, neg)
    emax = jnp.max(el, axis=-1, keepdims=True)
    eexp = jnp.where(in_grp, jnp.exp(el - emax), zero)
    ep = eexp / jnp.sum(eexp, axis=-1, keepdims=True)
    minus = jnp.full(shape, -1.0, F32)
    ep1 = jnp.where(in_grp, ep, minus)
    p1 = jnp.max(ep1, axis=-1, keepdims=True)
    i1 = jnp.min(jnp.where(ep1 == p1, lane, big), axis=-1, keepdims=True)
    ep2 = jnp.where(lane == i1, minus, ep1)
    p2 = jnp.max(ep2, axis=-1, keepdims=True)
    i2 = jnp.min(jnp.where(ep2 == p2, lane, big), axis=-1, keepdims=True)
    den = p1 + p2
    cmb = g_w * (jnp.where(lane == i1, p1 / den, zero) + jnp.where(lane == i2, p2 / den, zero))
    return cmb, jnp.where(lane == g_i, 1.0, 0.0)


def _moe_kernel(u_ref, lg_ref, h1_ref, tri_ref, wg_ref, wu_ref, wd_ref, fw_ref, o_ref,
                xs_scr, ys_scr, c8_scr, pos_scr, meta_ref):
    g = pl.program_id(1)
    tm = MOE_TM

    @pl.when(g == 0)
    def _():
        cmb, gsel = _route(lg_ref[...])
        lane1 = lax.broadcasted_iota(jnp.int32, (1, LANES), 1)
        rank = jnp.dot(tri_ref[...], gsel.astype(BF16), preferred_element_type=F32)
        cnt = jnp.sum(gsel, axis=0, keepdims=True)
        nch = jnp.floor((cnt + (MOE_CHUNK - 1.0)) * (1.0 / MOE_CHUNK))
        n0, n1, n2 = nch[:, 0:1], nch[:, 1:2], nch[:, 2:3]
        base = jnp.where(lane1 == 1, n0, jnp.where(lane1 == 2, n0 + n1, jnp.where(lane1 == 3, n0 + n1 + n2, 0.0)))
        for gi in range(N_GROUPS):
            meta_ref[gi] = jnp.sum(jnp.where(lane1 == gi, nch, 0.0)).astype(jnp.int32)
            meta_ref[N_GROUPS + gi] = jnp.sum(jnp.where(lane1 == gi, base, 0.0)).astype(jnp.int32)
        pos = jnp.sum(gsel * (rank + base * MOE_CHUNK), axis=-1, keepdims=True)
        pos_b = jnp.broadcast_to(pos, (tm, LANES))
        pos_scr[...] = pos_b
        pos_row = pos_b.T[0:1, :]
        rowi = lax.broadcasted_iota(jnp.int32, (MOE_CAP, tm), 0).astype(F32)
        perm = jnp.where(rowi == pos_row, 1.0, 0.0).astype(BF16)
        xs_scr[...] = jnp.dot(perm, u_ref[...], preferred_element_type=F32).astype(BF16)
        lrow = lax.broadcasted_iota(jnp.int32, (LANES, LANES), 0) - N_GROUPS
        lcol = lax.broadcasted_iota(jnp.int32, (LANES, LANES), 1)
        fold = jnp.where((lrow >= 0) & (lrow < N_EXPERTS) & ((lrow % EXPERTS_PER_GROUP) == lcol), 1.0, 0.0).astype(BF16)
        c_hi, c_lo = _split_bf16(cmb)
        c8 = jnp.dot(c_hi, fold, preferred_element_type=F32) + jnp.dot(c_lo, fold, preferred_element_type=F32)
        c8_hi, c8_lo = _split_bf16(c8)
        c8_scr[...] = (jnp.dot(perm, c8_hi, preferred_element_type=F32)
                       + jnp.dot(perm, c8_lo, preferred_element_type=F32))
        ys_scr[...] = jnp.zeros(ys_scr.shape, BF16)

    base_g = meta_ref[N_GROUPS + g]

    def chunk(jc, carry):
        r0 = pl.multiple_of((base_g + jc) * MOE_CHUNK, MOE_CHUNK)
        xc = xs_scr[pl.ds(r0, MOE_CHUNK), :]
        cw = c8_scr[pl.ds(r0, MOE_CHUNK), :]
        parts = []
        for e in range(EXPERTS_PER_GROUP):
            a = jnp.dot(xc, wg_ref[e], preferred_element_type=F32)
            b = jnp.dot(xc, wu_ref[e], preferred_element_type=F32)
            parts.append((_silu(a) * b * cw[:, e:e + 1]).astype(BF16))
        hh = jnp.concatenate(parts, axis=1)
        wd = wd_ref[...].reshape(EXPERTS_PER_GROUP * D_EXPERT, D_MODEL)
        ys_scr[pl.ds(r0, MOE_CHUNK), :] = jnp.dot(hh, wd, preferred_element_type=F32).astype(BF16)
        return carry

    lax.fori_loop(0, meta_ref[g], chunk, 0)

    @pl.when(g == N_GROUPS - 1)
    def _():
        pos_w = jnp.concatenate([pos_scr[...]] * (MOE_CAP // LANES), axis=1)
        lane = lax.broadcasted_iota(jnp.int32, (tm, MOE_CAP), 1).astype(F32)
        unperm = jnp.where(lane == pos_w, 1.0, 0.0).astype(BF16)
        hf = h1_ref[...] + jnp.dot(unperm, ys_scr[...], preferred_element_type=F32)
        o_ref[0] = hf * lax.rsqrt(jnp.mean(hf * hf, axis=-1, keepdims=True) + NORM_EPS) * fw_ref[...]


def _moe(u2, logits, h1, tri, w_gate, w_up, w_down, final_w):
    seq = u2.shape[0]
    once = pl.Buffered(1)
    row = lambda w, **kw: pl.BlockSpec((MOE_TM, w), lambda m, g: (m, 0), **kw)
    return pl.pallas_call(
        _moe_kernel,
        grid=(seq // MOE_TM, N_GROUPS),
        in_specs=[
            row(D_MODEL), row(LANES), row(D_MODEL, pipeline_mode=once),
            pl.BlockSpec((MOE_TM, MOE_TM), lambda m, g: (0, 0), pipeline_mode=once),
            pl.BlockSpec((EXPERTS_PER_GROUP, D_MODEL, D_EXPERT), lambda m, g: (g, 0, 0)),
            pl.BlockSpec((EXPERTS_PER_GROUP, D_MODEL, D_EXPERT), lambda m, g: (g, 0, 0)),
            pl.BlockSpec((EXPERTS_PER_GROUP, D_EXPERT, D_MODEL), lambda m, g: (g, 0, 0)),
            pl.BlockSpec((1, D_MODEL), lambda m, g: (0, 0)),
        ],
        out_specs=pl.BlockSpec((1, MOE_TM, D_MODEL), lambda m, g: (0, m, 0), pipeline_mode=once),
        out_shape=jax.ShapeDtypeStruct((1, seq, D_MODEL), F32),
        scratch_shapes=[pltpu.VMEM((MOE_CAP, D_MODEL), BF16), pltpu.VMEM((MOE_CAP, D_MODEL), BF16),
                        pltpu.VMEM((MOE_CAP, LANES), F32), pltpu.VMEM((MOE_TM, LANES), F32),
                        pltpu.SMEM((2 * N_GROUPS,), jnp.int32)],
        compiler_params=pltpu.CompilerParams(dimension_semantics=("parallel", "arbitrary"),
                                             vmem_limit_bytes=MOE_VMEM_LIMIT),
        name="moe_final",
    )(u2, logits, h1, tri, w_gate, w_up, w_down, final_w)


def kernel(x, meta_tokens, norm1_w, w_in, lambda_q1, lambda_k1, lambda_q2, lambda_k2, da_subln_w, dn_conv_w,
           dn_A_log, dn_dt_bias, dn_norm_w, w_branch_attn, w_branch_dn, w_out, norm2_w, router_group_w,
           router_group_b, router_expert_w, router_expert_b, expert_w_gate, expert_w_up, expert_w_down,
           final_norm_w):
    assert x.shape[0] == 1 and x.shape[2] == D_MODEL and x.shape[1] % ATT_TQ == 0
    assert w_in.shape[0] == 1, "single layer: the block output at the meta rows is unused"
    seq = x.shape[1]
    lt = TILE + seq
    assert lt <= 256 * LANES, "key positions must split into two exact bf16 terms"
    first = TILE - N_META

    x2d = x[0]
    meta_tile = jnp.concatenate([jnp.zeros((first, D_MODEL), F32), meta_tokens.astype(F32)], axis=0)

    w = w_in[0]
    d = D_MODEL
    w_a = w[:, 3 * d:6 * d].astype(BF16)
    w_b = jnp.concatenate([w[:, d:2 * d], w[:, 6 * d:7 * d], w[:, 7 * d + 2 * N_HEADS:]], axis=1).astype(BF16)
    w_qv_t = jnp.concatenate([w[:, :d] * DA_HEAD_DIM ** -0.5, w[:, 2 * d:3 * d]], axis=1).T.astype(BF16)
    w_small = jnp.pad(w[:, 7 * d:7 * d + 2 * N_HEADS], ((0, 0), (0, LANES - 2 * N_HEADS)))
    pad16 = lambda t, off: jnp.pad(t, ((0, 0), (off, LANES - N_HEADS - off)))
    proj, qvt, qn, kn, vc, bg = _inproj(x2d, meta_tile, norm1_w, w_a, w_b, w_qv_t, w_small, dn_conv_w[0],
                                        pad16(dn_A_log, N_HEADS), pad16(dn_dt_bias, N_HEADS), first, lt)

    pos = jnp.arange(lt, dtype=jnp.int32)
    kpos = jnp.stack([(pos >> 7) << 7, pos & 127, jnp.ones_like(pos)], axis=1).astype(BF16)
    kpos = jnp.pad(kpos, ((0, 0), (0, LANES - 3)))
    slopes = jnp.broadcast_to(jnp.asarray(ALIBI_SLOPES, F32)[:, None], (N_HEADS, LANES))
    lam_tab = jnp.pad(jnp.concatenate([lambda_q1, lambda_k1, lambda_q2, lambda_k2], axis=0),
                      ((0, 4), (0, LANES - DA_HEAD_DIM)))
    y_a = _attention(qvt, proj, kpos, slopes, lam_tab, da_subln_w, seq, first)

    y_d = _dn_chunks(qn, kn, vc, bg, proj, dn_norm_w)

    w_r = jnp.pad(jnp.concatenate([router_group_w[0], router_expert_w[0]], axis=1),
                  ((0, 0), (0, LANES - N_GROUPS - N_EXPERTS)))
    b_r = jnp.pad(jnp.concatenate([router_group_b, router_expert_b], axis=1),
                  ((0, 0), (0, LANES - N_GROUPS - N_EXPERTS)))
    h1, u2, logits = _mix(y_a, y_d, proj, x2d, w_branch_attn[0].astype(BF16), w_branch_dn[0].astype(BF16),
                          w_out[0].astype(BF16), norm2_w, w_r, b_r)

    tri = jnp.tril(jnp.ones((MOE_TM, MOE_TM), BF16), -1)
    out = _moe(u2, logits, h1, tri, expert_w_gate[0].astype(BF16), expert_w_up[0].astype(BF16),
               expert_w_down[0].astype(BF16), final_norm_w[None, :])
    return out
```

```python
import functools
import math

import jax
import jax.numpy as jnp
from jax import lax
from jax.experimental import pallas as pl
from jax.experimental.pallas import tpu as pltpu

F32 = jnp.float32
BF16 = jnp.bfloat16

D_MODEL = 1024
N_META = 16
NORM_EPS = 1e-6
N_HEADS = 8
HEAD_W = 128
DA_HEAD_DIM = 64
DN_CONV = 4
DN_CHUNK = 64
N_GROUPS = 4
EXPERTS_PER_GROUP = 8
N_EXPERTS = 32
D_EXPERT = 256
LAMBDA_INIT = 0.8 - 0.6 * math.exp(-0.3 * 0)
ALIBI_SLOPES = tuple(2.0 ** (-8.0 * (h + 1) / N_HEADS) for h in range(N_HEADS))

TILE = 512
ATT_TQ = 2048
Q_CHAIN = 256
V_ROWS = 144
DN_ROWS = 256
DN_HEADS_PER_STEP = 8
MIX_SUB = 2
LANES = 128
NEG = -1e30
VMEM_LIMIT = 56 * 1024 * 1024
MOE_TM = 1024
MOE_CHUNK = 128
MOE_CAP = MOE_TM + N_GROUPS * MOE_CHUNK
MOE_VMEM_LIMIT = 60 * 1024 * 1024

CB_AK, CB_DG, CB_GA, CB_GD = 0, 8, 16, 24
PROJ_COLS = 32 * 128


def _cparams(sem):
    return pltpu.CompilerParams(dimension_semantics=sem, vmem_limit_bytes=VMEM_LIMIT)


def _mm(a, b):
    return jnp.dot(a.astype(BF16), b.astype(BF16), preferred_element_type=F32)


def _mm_nt(a, b):
    return lax.dot_general(a.astype(BF16), b.astype(BF16), (((1,), (1,)), ((), ())),
                           preferred_element_type=F32)


def _mm_tn(a, b):
    return lax.dot_general(a.astype(BF16), b.astype(BF16), (((0,), (0,)), ((), ())),
                           preferred_element_type=F32)


def _split_bf16(x):
    hi = x.astype(BF16)
    return hi, (x - hi.astype(F32)).astype(BF16)


def _mm_f32(a, b):
    a_hi, a_lo = _split_bf16(a)
    b_hi, b_lo = _split_bf16(b)
    dot = functools.partial(jnp.dot, preferred_element_type=F32)
    return dot(a_hi, b_hi) + (dot(a_hi, b_lo) + dot(a_lo, b_hi))


def _silu(x):
    return x * (1.0 / (1.0 + jnp.exp(-x)))


def _sigmoid(x):
    return 1.0 / (1.0 + jnp.exp(-x))


def _inproj_kernel(x_ref, meta_ref, nw_ref, wa_ref, wb_ref, wt_ref, ws_ref, cw_ref, al_ref, dtb_ref,
                   o_ref, ot_ref, q_ref, k_ref, v_ref, bg_ref, xe_scr, *, first_row, last_row):
    m = pl.program_id(0)
    tm = TILE

    @pl.when(m == 0)
    def _():
        xe_scr[0:8, :] = jnp.zeros((8, xe_scr.shape[1]), F32)

    x = jnp.where(m == 0, meta_ref[...], x_ref[...])
    u = x * lax.rsqrt(jnp.mean(x * x, axis=-1, keepdims=True) + NORM_EPS) * nw_ref[...]
    ub = u.astype(BF16)
    xe_scr[8:, :] = jnp.dot(ub, wa_ref[...], preferred_element_type=F32)
    o_ref[...] = jnp.dot(ub, wb_ref[...], preferred_element_type=F32).astype(BF16)
    ot_ref[0] = lax.dot_general(wt_ref[...], ub, (((1,), (1,)), ((), ())),
                                preferred_element_type=F32).astype(BF16)
    sm = _mm_f32(u, ws_ref[...])

    cw = cw_ref[...]
    acc = xe_scr[8:, :] * cw[DN_CONV - 1:DN_CONV, :]
    for j in range(DN_CONV - 1):
        acc = acc + xe_scr[pl.ds(8 - (DN_CONV - 1) + j, tm), :] * cw[j:j + 1, :]
    xe_scr[0:8, :] = xe_scr[tm:tm + 8, :]
    y = _silu(acc)

    width = N_HEADS * HEAD_W
    for h in range(N_HEADS):
        qh = y[:, h * HEAD_W:(h + 1) * HEAD_W]
        kh = y[:, width + h * HEAD_W: width + (h + 1) * HEAD_W]
        vh = y[:, 2 * width + h * HEAD_W: 2 * width + (h + 1) * HEAD_W]
        qh = qh * lax.rsqrt(jnp.sum(qh * qh, axis=-1, keepdims=True) + 1e-6) * (HEAD_W ** -0.5)
        kh = kh * lax.rsqrt(jnp.sum(kh * kh, axis=-1, keepdims=True) + 1e-6)
        q_ref[h] = qh.astype(BF16)
        k_ref[h] = kh.astype(BF16)
        v_ref[h] = vh.astype(BF16)

    rowid = m * tm + lax.broadcasted_iota(jnp.int32, (tm, LANES), 0)
    lane = lax.broadcasted_iota(jnp.int32, (tm, LANES), 1)
    real = (rowid >= first_row) & (rowid < last_row)
    a = sm + dtb_ref[...]
    softplus = jnp.maximum(a, 0.0) + jnp.log(1.0 + jnp.exp(-jnp.abs(a)))
    g_all = -jnp.exp(al_ref[...]) * softplus
    bg_ref[...] = jnp.where(real, jnp.where(lane < N_HEADS, _sigmoid(sm), g_all), 0.0)


def _inproj(x2d, meta_tile, norm_w, w_a, w_b, w_qv_t, w_small, conv_w, a_log_row, dtb_row, first_row, last_row):
    lt = TILE + x2d.shape[0]
    n_t = lt // TILE
    c = 3 * N_HEADS * HEAD_W
    once = pl.Buffered(1)
    const = lambda shape: pl.BlockSpec(shape, lambda m: (0,) * len(shape), pipeline_mode=once)
    kern = functools.partial(_inproj_kernel, first_row=first_row, last_row=last_row)
    hshape = jax.ShapeDtypeStruct((N_HEADS, lt, HEAD_W), BF16)
    hspec = pl.BlockSpec((N_HEADS, TILE, HEAD_W), lambda m: (0, m, 0))
    return pl.pallas_call(
        kern,
        grid=(n_t,),
        in_specs=[
            pl.BlockSpec((TILE, D_MODEL), lambda m: (jnp.maximum(m - 1, 0), 0)),
            const((TILE, D_MODEL)), const((1, D_MODEL)), const((D_MODEL, c)), const((D_MODEL, PROJ_COLS)),
            const((2 * D_MODEL, D_MODEL)), const((D_MODEL, LANES)), const((DN_CONV, c)),
            const((1, LANES)), const((1, LANES)),
        ],
        out_specs=[
            pl.BlockSpec((TILE, PROJ_COLS), lambda m: (m, 0)),
            pl.BlockSpec((1, 2 * D_MODEL, TILE), lambda m: (m, 0, 0)),
            hspec, hspec, hspec,
            pl.BlockSpec((TILE, LANES), lambda m: (m, 0)),
        ],
        out_shape=[
            jax.ShapeDtypeStruct((lt, PROJ_COLS), BF16),
            jax.ShapeDtypeStruct((n_t, 2 * D_MODEL, TILE), BF16),
            hshape, hshape, hshape,
            jax.ShapeDtypeStruct((lt, LANES), F32),
        ],
        scratch_shapes=[pltpu.VMEM((TILE + 8, c), F32)],
        compiler_params=pltpu.CompilerParams(dimension_semantics=("arbitrary",), vmem_limit_bytes=MOE_VMEM_LIMIT),
        name="inproj",
    )(x2d, meta_tile, norm_w, w_a, w_b, w_qv_t, w_small, conv_w, a_log_row, dtb_row)


def _attn_kernel(*refs, tq, first_key):
    tpq = tq // TILE
    q_refs = refs[:tpq]
    k_ref, kpos_ref, vt_ref, slope_ref, lam_ref, sw_ref, o_ref, m_ref, a_ref, s_ref = refs[tpq:]
    h = pl.program_id(0)
    j = pl.program_id(1)
    tk = TILE
    n_q = tq // Q_CHAIN

    qt = jnp.concatenate([r[0] for r in q_refs], axis=1)
    slope = slope_ref[pl.ds(h, 1), :][:, :1]
    qbase = TILE + j * tq
    rowi = lax.broadcasted_iota(jnp.int32, (HEAD_W, tq), 0)
    brow = jnp.where(rowi < 2, slope, jnp.where(rowi == 2, -(slope * qbase.astype(F32)), 0.0)).astype(BF16)
    z64 = jnp.zeros((DA_HEAD_DIM, tq), BF16)
    w1 = jnp.concatenate([qt[:DA_HEAD_DIM], z64, brow], axis=0)
    w2 = jnp.concatenate([z64, qt[DA_HEAD_DIM:], brow], axis=0)
    chains = [(w, hf) for w in (w1, w2) for hf in range(n_q)]
    ones_rows = jnp.where(lax.broadcasted_iota(jnp.int32, (V_ROWS - HEAD_W, tk), 0) == 0, 1.0, 0.0).astype(BF16)

    m_ref[...] = jnp.full(m_ref.shape, NEG, F32)
    a_ref[...] = jnp.zeros(a_ref.shape, F32)

    def stage(ki, modes, lo=0, rows=tk, buf=0):
        start = pl.multiple_of(ki * tk, tk) + lo
        kt = jnp.concatenate([k_ref[pl.ds(start, rows), :], kpos_ref[pl.ds(start, rows), :]], axis=1)
        vt = jnp.concatenate([vt_ref[ki][:, lo:lo + rows], ones_rows[:, :rows]], axis=0)
        key = start + lax.broadcasted_iota(jnp.int32, (rows, Q_CHAIN), 0)
        lane = lax.broadcasted_iota(jnp.int32, (rows, Q_CHAIN), 1)
        tile_max = {}

        def score(c):
            w, hf = chains[c]
            st = jnp.dot(kt, w[:, hf * Q_CHAIN:(hf + 1) * Q_CHAIN], preferred_element_type=F32)
            if modes[hf] == "front":
                st = jnp.where(key >= first_key, st, NEG)
            elif modes[hf] == "causal":
                st = jnp.where(key <= lane + (qbase + hf * Q_CHAIN), st, NEG)
            s_ref[buf, c, :rows] = st
            tile_max[c] = jnp.max(st, axis=0, keepdims=True)

        def accumulate(c):
            m_old = m_ref[c]
            m_new = jnp.maximum(m_old, tile_max[c])
            alpha = jnp.exp(m_old - m_new)
            p = jnp.exp(s_ref[buf, c, :rows] - m_new).astype(BF16)
            a_ref[c] = alpha * a_ref[c] + jnp.dot(vt, p, preferred_element_type=F32)
            m_ref[c] = m_new

        live = [c for c, (_, hf) in enumerate(chains) if modes[hf] != "skip"]
        return live, score, accumulate

    def tile(ki, modes, **kw):
        live, score, accumulate = stage(ki, modes, **kw)
        for c in live:
            score(c)
        for c in live:
            accumulate(c)

    tile(0, ["front"] * n_q, lo=tk - LANES, rows=LANES)

    def body(pair, carry):
        live, score_a, acc_a = stage(2 * pair + 1, ["none"] * n_q, buf=0)
        _, score_b, acc_b = stage(2 * pair + 2, ["none"] * n_q, buf=1)
        for c in live:
            score_a(c)
        for c in live:
            acc_a(c)
            score_b(c)
        for c in live:
            acc_b(c)
        return carry

    assert tpq % 2 == 0
    lax.fori_loop(0, (tpq // 2) * j, body, 0)
    for d in range(tpq):
        modes = ["skip" if (hf + 1) * Q_CHAIN <= d * tk else "none" if hf * Q_CHAIN >= (d + 1) * tk else "causal"
                 for hf in range(n_q)]
        tile(tpq * j + 1 + d, modes)

    lt = lam_ref[...]
    lam = (jnp.exp(jnp.sum(lt[0:1] * lt[1:2], axis=-1, keepdims=True))
           - jnp.exp(jnp.sum(lt[2:3] * lt[3:4], axis=-1, keepdims=True)) + LAMBDA_INIT)
    for hf in range(n_q):
        a1 = a_ref[hf]
        a2 = a_ref[n_q + hf]
        ot = a1[:HEAD_W] / a1[HEAD_W:HEAD_W + 1] - lam * (a2[:HEAD_W] / a2[HEAD_W:HEAD_W + 1])
        o = ot.T
        o = o * lax.rsqrt(jnp.mean(o * o, axis=-1, keepdims=True) + NORM_EPS) * sw_ref[...]
        o_ref[hf * Q_CHAIN:(hf + 1) * Q_CHAIN, :] = (o * (1.0 - LAMBDA_INIT)).astype(BF16)


def _attention(qvt, proj, kpos, slopes, lam_tab, subln_w, seq, first_key):
    n_t = qvt.shape[0]
    lt = n_t * TILE
    tq = ATT_TQ
    tpq = tq // TILE
    assert tq % TILE == 0 and seq % tq == 0 and first_key >= TILE - LANES
    kern = functools.partial(_attn_kernel, tq=tq, first_key=first_key)
    n_chain = 2 * tq // Q_CHAIN
    scratch = [pltpu.VMEM((n_chain, 1, Q_CHAIN), F32), pltpu.VMEM((n_chain, V_ROWS, Q_CHAIN), F32),
               pltpu.VMEM((2, n_chain, TILE, Q_CHAIN), F32)]
    q_spec = lambda d: pl.BlockSpec((1, HEAD_W, TILE), lambda h, j: (1 + d + tpq * j, h, 0))
    once = pl.Buffered(1)
    return pl.pallas_call(
        kern,
        grid=(N_HEADS, seq // tq),
        in_specs=[q_spec(d) for d in range(tpq)] + [
            pl.BlockSpec((lt, HEAD_W), lambda h, j: (0, CB_AK + h), pipeline_mode=once),
            pl.BlockSpec((lt, LANES), lambda h, j: (0, 0), pipeline_mode=once),
            pl.BlockSpec((n_t, HEAD_W, TILE), lambda h, j: (0, N_HEADS + h, 0), pipeline_mode=once),
            pl.BlockSpec((8, LANES), lambda h, j: (0, 0)),
            pl.BlockSpec((8, LANES), lambda h, j: (0, 0)),
            pl.BlockSpec((1, HEAD_W), lambda h, j: (0, 0)),
        ],
        out_specs=pl.BlockSpec((tq, HEAD_W), lambda h, j: (j, h)),
        out_shape=jax.ShapeDtypeStruct((seq, N_HEADS * HEAD_W), BF16),
        scratch_shapes=scratch,
        compiler_params=_cparams(("parallel", "arbitrary")),
        name="diff_attn",
    )(*([qvt] * tpq), proj, kpos, qvt, slopes, lam_tab, subln_w)


def _dn_chunk_kernel(q_ref, k_ref, v_ref, bg_ref, gate_ref, nw_ref, o_ref, s_scr):
    @pl.when(pl.program_id(1) == 0)
    def _():
        s_scr[...] = jnp.zeros(s_scr.shape, F32)

    r = DN_ROWS
    c = DN_CHUNK
    n_chunks = r // c
    hs = range(q_ref.shape[0])
    h0 = pl.program_id(0) * q_ref.shape[0]

    row = lax.broadcasted_iota(jnp.int32, (r, r), 0)
    col = lax.broadcasted_iota(jnp.int32, (r, r), 1)
    same64 = (row // c) == (col // c)
    same32 = (row // 32) == (col // 32)
    same16 = (row // 16) == (col // 16)
    incl = same64 & (row >= col)
    strict = same64 & (row > col)
    off16 = same32 & jnp.logical_not(same16)
    off32 = same64 & jnp.logical_not(same32)
    zero = jnp.zeros((r, r), F32)
    incl_b = jnp.where(incl, 1.0, 0.0).astype(BF16)

    q = [q_ref[h].astype(F32) for h in hs]
    k = [k_ref[h].astype(F32) for h in hs]
    v = [v_ref[h].astype(F32) for h in hs]
    bg = bg_ref[...]
    lane = lax.broadcasted_iota(jnp.int32, bg.shape, 1)
    pick = lambda l: jnp.broadcast_to(jnp.sum(jnp.where(lane == l, bg, 0.0), axis=-1, keepdims=True), bg.shape)
    beta = [pick(h0 + h) for h in hs]
    g_in = [pick(N_HEADS + h0 + h) for h in hs]

    def cumsum(g):
        g_hi = g.astype(BF16)
        r1 = g - g_hi.astype(F32)
        g_mid = r1.astype(BF16)
        g_lo = (r1 - g_mid.astype(F32)).astype(BF16)
        parts = jnp.dot(incl_b, jnp.concatenate([g_hi, g_mid, g_lo], axis=1), preferred_element_type=F32)
        return (parts[:, :LANES] + parts[:, LANES:2 * LANES]) + parts[:, 2 * LANES:]

    gc = [cumsum(g_in[h]) for h in hs]
    gc_row = [gc[h].T[0:1, :] for h in hs]
    diff = [jnp.concatenate([gc[h], gc[h]], axis=1) - gc_row[h] for h in hs]
    decay = [jnp.where(incl, jnp.exp(jnp.where(incl, diff[h], zero)), zero) for h in hs]
    gtot = [jnp.concatenate([jnp.broadcast_to(gc[h][(i + 1) * c - 1:(i + 1) * c, :], (c, LANES))
                             for i in range(n_chunks)], axis=0) for h in hs]
    egc = [jnp.exp(gc[h]) for h in hs]
    kb = [k[h] * beta[h] for h in hs]
    a = [jnp.where(strict, _mm_nt(kb[h], k[h]) * decay[h], zero).astype(BF16) for h in hs]

    zero_b = jnp.zeros((r, r), BF16)
    diag = row == col
    plus_eye = lambda p: jnp.where(diag, 1.0, p).astype(BF16)
    ad = [jnp.where(same16, a[h], zero_b) for h in hs]
    p2 = [_mm(ad[h], ad[h]).astype(BF16) for h in hs]
    p4 = [_mm(p2[h], p2[h]).astype(BF16) for h in hs]
    p8 = [_mm(p4[h], p4[h]) for h in hs]
    t = [_mm(plus_eye(-ad[h]), plus_eye(p2[h])).astype(BF16) for h in hs]
    t = [_mm(t[h], plus_eye(p4[h])).astype(BF16) for h in hs]
    t = [_mm(t[h], plus_eye(p8[h])) for h in hs]
    x = [_mm(t[h], jnp.where(off16, a[h], zero_b)) for h in hs]
    t = [t[h] - _mm(x[h], t[h]) for h in hs]
    x = [_mm(t[h], jnp.where(off32, a[h], zero_b)) for h in hs]
    t = [t[h] - _mm(x[h], t[h]) for h in hs]

    uw = [_mm(t[h], jnp.concatenate([v[h] * beta[h], kb[h] * egc[h]], axis=1)) for h in hs]
    qk = [jnp.where(incl, _mm_nt(q[h], k[h]) * decay[h], zero) for h in hs]
    qd = [q[h] * egc[h] for h in hs]
    kd = [k[h] * jnp.exp(gtot[h] - gc[h]) for h in hs]
    gts = [jnp.exp(gtot[h]) for h in hs]

    s = [s_scr[h] for h in hs]
    vns = [[] for _ in hs]
    outs = [[] for _ in hs]
    for i in range(n_chunks):
        rs = slice(i * c, (i + 1) * c)
        vn = [uw[h][rs, :HEAD_W] - _mm(uw[h][rs, HEAD_W:], s[h]) for h in hs]
        for h in hs:
            vns[h].append(vn[h])
        pad = [jnp.zeros(((n_chunks - 1 - i) * c, HEAD_W), F32)] * (i < n_chunks - 1)
        vn_all = [jnp.concatenate(vns[h] + pad, axis=0) for h in hs]
        o_i = [_mm(qd[h][rs], s[h]) + _mm(qk[h][rs], vn_all[h]) for h in hs]
        for h in hs:
            outs[h].append(o_i[h])
        s = [s[h] * gts[h][i * c:i * c + 1, :] + _mm_tn(kd[h][rs], vn[h]) for h in hs]
    for h in hs:
        s_scr[h] = s[h]

    for h in hs:
        o = jnp.concatenate(outs[h], axis=0)
        o = o * lax.rsqrt(jnp.mean(o * o, axis=-1, keepdims=True) + NORM_EPS) * nw_ref[...]
        gate = gate_ref[:, h * HEAD_W:(h + 1) * HEAD_W].astype(F32)
        o_ref[:, h * HEAD_W:(h + 1) * HEAD_W] = (o * _silu(gate)).astype(BF16)


def _dn_chunks(qn, kn, vc, bg, proj, norm_w):
    lt = qn.shape[1]
    hg = DN_HEADS_PER_STEP
    first_blk = (TILE - DN_ROWS) // DN_ROWS
    x_blk = TILE // DN_ROWS - first_blk
    hspec = pl.BlockSpec((hg, DN_ROWS, HEAD_W), lambda g, i: (g, i + first_blk, 0))
    wide = hg * HEAD_W
    return pl.pallas_call(
        _dn_chunk_kernel,
        grid=(N_HEADS // hg, lt // DN_ROWS - first_blk),
        in_specs=[
            hspec, hspec, hspec,
            pl.BlockSpec((DN_ROWS, LANES), lambda g, i: (i + first_blk, 0)),
            pl.BlockSpec((DN_ROWS, wide), lambda g, i: (i + first_blk, CB_DG * HEAD_W // wide + g)),
            pl.BlockSpec((1, HEAD_W), lambda g, i: (0, 0)),
        ],
        out_specs=pl.BlockSpec((DN_ROWS, wide), lambda g, i: (jnp.maximum(i - x_blk, 0), g)),
        out_shape=jax.ShapeDtypeStruct((lt - TILE, N_HEADS * HEAD_W), BF16),
        scratch_shapes=[pltpu.VMEM((hg, HEAD_W, HEAD_W), F32)],
        compiler_params=_cparams(("parallel", "arbitrary")),
        name="dn_chunks",
    )(qn, kn, vc, bg, proj, norm_w)


def _mix_kernel(ya_ref, yd_ref, ga_ref, gd_ref, h_ref, wa_ref, wd_ref, wo_ref, nw_ref, wr_ref, br_ref,
                h1_ref, u2_ref, lg_ref):
    n_sub = MIX_SUB
    rs = [slice(i * (TILE // n_sub), (i + 1) * (TILE // n_sub)) for i in range(n_sub)]
    pa = [jnp.dot(ya_ref[r, :], wa_ref[...], preferred_element_type=F32) for r in rs]
    pd = [jnp.dot(yd_ref[r, :], wd_ref[...], preferred_element_type=F32) for r in rs]
    mixed = [_sigmoid(ga_ref[r, :].astype(F32)) * pa[i] + _sigmoid(gd_ref[r, :].astype(F32)) * pd[i]
             for i, r in enumerate(rs)]
    h1 = [h_ref[r, :] + jnp.dot(mixed[i].astype(BF16), wo_ref[...], preferred_element_type=F32)
          for i, r in enumerate(rs)]
    u2 = [h * lax.rsqrt(jnp.mean(h * h, axis=-1, keepdims=True) + NORM_EPS) * nw_ref[...] for h in h1]
    for i, r in enumerate(rs):
        h1_ref[r, :] = h1[i]
        u2_ref[r, :] = u2[i].astype(BF16)
        lg_ref[r, :] = _mm_f32(u2[i], wr_ref[...]) + br_ref[...]


def _mix(y_a, y_d, proj, x2d, w_a, w_d, w_o, norm_w, w_r, b_r):
    seq = y_a.shape[0]
    skip = (proj.shape[0] - seq) // TILE
    row = lambda: pl.BlockSpec((TILE, D_MODEL), lambda m: (m, 0))
    prow = lambda cb: pl.BlockSpec((TILE, D_MODEL), lambda m: (m + skip, cb))
    full = lambda: pl.BlockSpec((D_MODEL, D_MODEL), lambda m: (0, 0))
    return pl.pallas_call(
        _mix_kernel,
        grid=(seq // TILE,),
        in_specs=[
            row(), row(), prow(CB_GA // 8), prow(CB_GD // 8), row(), full(), full(), full(),
            pl.BlockSpec((1, D_MODEL), lambda m: (0, 0)),
            pl.BlockSpec((D_MODEL, LANES), lambda m: (0, 0)),
            pl.BlockSpec((1, LANES), lambda m: (0, 0)),
        ],
        out_specs=[row(), row(), pl.BlockSpec((TILE, LANES), lambda m: (m, 0))],
        out_shape=[
            jax.ShapeDtypeStruct((seq, D_MODEL), F32),
            jax.ShapeDtypeStruct((seq, D_MODEL), BF16),
            jax.ShapeDtypeStruct((seq, LANES), F32),
        ],
        compiler_params=_cparams(("parallel",)),
        name="branch_mix",
    )(y_a, y_d, proj, proj, x2d, w_a, w_d, w_o, norm_w, w_r, b_r)


def _route(logits):
    shape = logits.shape
    lane = lax.broadcasted_iota(jnp.int32, shape, 1)
    big = jnp.full(shape, 4 * LANES, jnp.int32)
    neg = jnp.full(shape, NEG, F32)
    zero = jnp.zeros(shape, F32)
    is_g = lane < N_GROUPS
    gl = jnp.where(is_g, logits, neg)
    gmax = jnp.max(gl, axis=-1, keepdims=True)
    gsum = jnp.sum(jnp.where(is_g, jnp.exp(gl - gmax), zero), axis=-1, keepdims=True)
    g_w = 1.0 / gsum
    g_i = jnp.min(jnp.where(is_g & (gl == gmax), lane, big), axis=-1, keepdims=True)
    e_lane = lane - N_GROUPS
    in_grp = (e_lane >= 0) & (e_lane < N_EXPERTS) & ((e_lane // EXPERTS_PER_GROUP) == g_i)
    el = jnp.where(in_grp, logits, neg)
    emax = jnp.max(el, axis=-1, keepdims=True)
    eexp = jnp.where(in_grp, jnp.exp(el - emax), zero)
    ep = eexp / jnp.sum(eexp, axis=-1, keepdims=True)
    minus = jnp.full(shape, -1.0, F32)
    ep1 = jnp.where(in_grp, ep, minus)
    p1 = jnp.max(ep1, axis=-1, keepdims=True)
    i1 = jnp.min(jnp.where(ep1 == p1, lane, big), axis=-1, keepdims=True)
    ep2 = jnp.where(lane == i1, minus, ep1)
    p2 = jnp.max(ep2, axis=-1, keepdims=True)
    i2 = jnp.min(jnp.where(ep2 == p2, lane, big), axis=-1, keepdims=True)
    den = p1 + p2
    cmb = g_w * (jnp.where(lane == i1, p1 / den, zero) + jnp.where(lane == i2, p2 / den, zero))
    return cmb, jnp.where(lane == g_i, 1.0, 0.0)


def _moe_kernel(u_ref, lg_ref, h1_ref, tri_ref, wg_ref, wu_ref, wd_ref, fw_ref, o_ref,
                xs_scr, ys_scr, c8_scr, pos_scr, meta_ref):
    g = pl.program_id(1)
    tm = MOE_TM

    @pl.when(g == 0)
    def _():
        cmb, gsel = _route(lg_ref[...])
        lane1 = lax.broadcasted_iota(jnp.int32, (1, LANES), 1)
        rank = jnp.dot(tri_ref[...], gsel.astype(BF16), preferred_element_type=F32)
        cnt = jnp.sum(gsel, axis=0, keepdims=True)
        nch = jnp.floor((cnt + (MOE_CHUNK - 1.0)) * (1.0 / MOE_CHUNK))
        n0, n1, n2 = nch[:, 0:1], nch[:, 1:2], nch[:, 2:3]
        base = jnp.where(lane1 == 1, n0, jnp.where(lane1 == 2, n0 + n1, jnp.where(lane1 == 3, n0 + n1 + n2, 0.0)))
        for gi in range(N_GROUPS):
            meta_ref[gi] = jnp.sum(jnp.where(lane1 == gi, nch, 0.0)).astype(jnp.int32)
            meta_ref[N_GROUPS + gi] = jnp.sum(jnp.where(lane1 == gi, base, 0.0)).astype(jnp.int32)
        pos = jnp.sum(gsel * (rank + base * MOE_CHUNK), axis=-1, keepdims=True)
        pos_b = jnp.broadcast_to(pos, (tm, LANES))
        pos_scr[...] = pos_b
        pos_row = pos_b.T[0:1, :]
        rowi = lax.broadcasted_iota(jnp.int32, (MOE_CAP, tm), 0).astype(F32)
        perm = jnp.where(rowi == pos_row, 1.0, 0.0).astype(BF16)
        xs_scr[...] = jnp.dot(perm, u_ref[...], preferred_element_type=F32).astype(BF16)
        lrow = lax.broadcasted_iota(jnp.int32, (LANES, LANES), 0) - N_GROUPS
        lcol = lax.broadcasted_iota(jnp.int32, (LANES, LANES), 1)
        fold = jnp.where((lrow >= 0) & (lrow < N_EXPERTS) & ((lrow % EXPERTS_PER_GROUP) == lcol), 1.0, 0.0).astype(BF16)
        c_hi, c_lo = _split_bf16(cmb)
        c8 = jnp.dot(c_hi, fold, preferred_element_type=F32) + jnp.dot(c_lo, fold, preferred_element_type=F32)
        c8_hi, c8_lo = _split_bf16(c8)
        c8_scr[...] = (jnp.dot(perm, c8_hi, preferred_element_type=F32)
                       + jnp.dot(perm, c8_lo, preferred_element_type=F32))
        ys_scr[...] = jnp.zeros(ys_scr.shape, BF16)

    base_g = meta_ref[N_GROUPS + g]

    def chunk(jc, carry):
        r0 = pl.multiple_of((base_g + jc) * MOE_CHUNK, MOE_CHUNK)
        xc = xs_scr[pl.ds(r0, MOE_CHUNK), :]
        cw = c8_scr[pl.ds(r0, MOE_CHUNK), :]
        parts = []
        for e in range(EXPERTS_PER_GROUP):
            a = jnp.dot(xc, wg_ref[e], preferred_element_type=F32)
            b = jnp.dot(xc, wu_ref[e], preferred_element_type=F32)
            parts.append((_silu(a) * b * cw[:, e:e + 1]).astype(BF16))
        hh = jnp.concatenate(parts, axis=1)
        wd = wd_ref[...].reshape(EXPERTS_PER_GROUP * D_EXPERT, D_MODEL)
        ys_scr[pl.ds(r0, MOE_CHUNK), :] = jnp.dot(hh, wd, preferred_element_type=F32).astype(BF16)
        return carry

    lax.fori_loop(0, meta_ref[g], chunk, 0)

    @pl.when(g == N_GROUPS - 1)
    def _():
        pos_w = jnp.concatenate([pos_scr[...]] * (MOE_CAP // LANES), axis=1)
        lane = lax.broadcasted_iota(jnp.int32, (tm, MOE_CAP), 1).astype(F32)
        unperm = jnp.where(lane == pos_w, 1.0, 0.0).astype(BF16)
        hf = h1_ref[...] + jnp.dot(unperm, ys_scr[...], preferred_element_type=F32)
        o_ref[0] = hf * lax.rsqrt(jnp.mean(hf * hf, axis=-1, keepdims=True) + NORM_EPS) * fw_ref[...]


def _moe(u2, logits, h1, tri, w_gate, w_up, w_down, final_w):
    seq = u2.shape[0]
    once = pl.Buffered(1)
    row = lambda w, **kw: pl.BlockSpec((MOE_TM, w), lambda m, g: (m, 0), **kw)
    return pl.pallas_call(
        _moe_kernel,
        grid=(seq // MOE_TM, N_GROUPS),
        in_specs=[
            row(D_MODEL), row(LANES), row(D_MODEL, pipeline_mode=once),
            pl.BlockSpec((MOE_TM, MOE_TM), lambda m, g: (0, 0), pipeline_mode=once),
            pl.BlockSpec((EXPERTS_PER_GROUP, D_MODEL, D_EXPERT), lambda m, g: (g, 0, 0)),
            pl.BlockSpec((EXPERTS_PER_GROUP, D_MODEL, D_EXPERT), lambda m, g: (g, 0, 0)),
            pl.BlockSpec((EXPERTS_PER_GROUP, D_EXPERT, D_MODEL), lambda m, g: (g, 0, 0)),
            pl.BlockSpec((1, D_MODEL), lambda m, g: (0, 0)),
        ],
        out_specs=pl.BlockSpec((1, MOE_TM, D_MODEL), lambda m, g: (0, m, 0), pipeline_mode=once),
        out_shape=jax.ShapeDtypeStruct((1, seq, D_MODEL), F32),
        scratch_shapes=[pltpu.VMEM((MOE_CAP, D_MODEL), BF16), pltpu.VMEM((MOE_CAP, D_MODEL), BF16),
                        pltpu.VMEM((MOE_CAP, LANES), F32), pltpu.VMEM((MOE_TM, LANES), F32),
                        pltpu.SMEM((2 * N_GROUPS,), jnp.int32)],
        compiler_params=pltpu.CompilerParams(dimension_semantics=("parallel", "arbitrary"),
                                             vmem_limit_bytes=MOE_VMEM_LIMIT),
        name="moe_final",
    )(u2, logits, h1, tri, w_gate, w_up, w_down, final_w)


def kernel(x, meta_tokens, norm1_w, w_in, lambda_q1, lambda_k1, lambda_q2, lambda_k2, da_subln_w, dn_conv_w,
           dn_A_log, dn_dt_bias, dn_norm_w, w_branch_attn, w_branch_dn, w_out, norm2_w, router_group_w,
           router_group_b, router_expert_w, router_expert_b, expert_w_gate, expert_w_up, expert_w_down,
           final_norm_w):
    assert x.shape[0] == 1 and x.shape[2] == D_MODEL and x.shape[1] % ATT_TQ == 0
    assert w_in.shape[0] == 1, "single layer: the block output at the meta rows is unused"
    seq = x.shape[1]
    lt = TILE + seq
    assert lt <= 256 * LANES, "key positions must split into two exact bf16 terms"
    first = TILE - N_META

    x2d = x[0]
    meta_tile = jnp.concatenate([jnp.zeros((first, D_MODEL), F32), meta_tokens.astype(F32)], axis=0)

    w = w_in[0]
    d = D_MODEL
    w_a = w[:, 3 * d:6 * d].astype(BF16)
    w_b = jnp.concatenate([w[:, d:2 * d], w[:, 6 * d:7 * d], w[:, 7 * d + 2 * N_HEADS:]], axis=1).astype(BF16)
    w_qv_t = jnp.concatenate([w[:, :d] * DA_HEAD_DIM ** -0.5, w[:, 2 * d:3 * d]], axis=1).T.astype(BF16)
    w_small = jnp.pad(w[:, 7 * d:7 * d + 2 * N_HEADS], ((0, 0), (0, LANES - 2 * N_HEADS)))
    pad16 = lambda t, off: jnp.pad(t, ((0, 0), (off, LANES - N_HEADS - off)))
    proj, qvt, qn, kn, vc, bg = _inproj(x2d, meta_tile, norm1_w, w_a, w_b, w_qv_t, w_small, dn_conv_w[0],
                                        pad16(dn_A_log, N_HEADS), pad16(dn_dt_bias, N_HEADS), first, lt)

    pos = jnp.arange(lt, dtype=jnp.int32)
    kpos = jnp.stack([(pos >> 7) << 7, pos & 127, jnp.ones_like(pos)], axis=1).astype(BF16)
    kpos = jnp.pad(kpos, ((0, 0), (0, LANES - 3)))
    slopes = jnp.broadcast_to(jnp.asarray(ALIBI_SLOPES, F32)[:, None], (N_HEADS, LANES))
    lam_tab = jnp.pad(jnp.concatenate([lambda_q1, lambda_k1, lambda_q2, lambda_k2], axis=0),
                      ((0, 4), (0, LANES - DA_HEAD_DIM)))
    y_a = _attention(qvt, proj, kpos, slopes, lam_tab, da_subln_w, seq, first)

    y_d = _dn_chunks(qn, kn, vc, bg, proj, dn_norm_w)

    w_r = jnp.pad(jnp.concatenate([router_group_w[0], router_expert_w[0]], axis=1),
                  ((0, 0), (0, LANES - N_GROUPS - N_EXPERTS)))
    b_r = jnp.pad(jnp.concatenate([router_group_b, router_expert_b], axis=1),
                  ((0, 0), (0, LANES - N_GROUPS - N_EXPERTS)))
    h1, u2, logits = _mix(y_a, y_d, proj, x2d, w_branch_attn[0].astype(BF16), w_branch_dn[0].astype(BF16),
                          w_out[0].astype(BF16), norm2_w, w_r, b_r)

    tri = jnp.tril(jnp.ones((MOE_TM, MOE_TM), BF16), -1)
    out = _moe(u2, logits, h1, tri, expert_w_gate[0].astype(BF16), expert_w_up[0].astype(BF16),
               expert_w_down[0].astype(BF16), final_norm_w[None, :])
    return out
```
